```python
import math
import jax, jax.numpy as jnp
from jax import lax
import numpy as np

D_MODEL = 4096
BATCH = 4
SEQ = 4096
DEPTH = 2

CHUNK = 64
Q_BLOCK = 128
A_HEADS = 12
A_NOPE = 128
A_ROPE = 64
A_VDIM = 128
A_Q_LORA = 768
A_KV_LORA = 512
S_GROUP_CH = 16
S_WIDTH = 1024
S_GROUPS = S_WIDTH // S_GROUP_CH
S_STATE = 64
S_DT_MIN = 1e-3
S_DT_MAX = 1e-1
C_HEADS = 12
C_HEAD_DIM = 128
I_HEADS = 16
I_DIM = 64
I_ROPE = 32
TOPK_MAX = 256
T5_BUCKETS = 32
T5_MAX_DIST = 128
D_FF = 4 * D_MODEL
ROPE_THETA = 10000.0
LN_EPS = 1e-5
RMS_EPS = 1e-6
DEEPNORM_ALPHA = (2 * DEPTH) ** 0.25
DEEPNORM_BETA = (8 * DEPTH) ** -0.25
MIX_WIDTH = A_HEADS * A_VDIM + S_WIDTH + C_HEADS * C_HEAD_DIM
IN_SIZES = (A_Q_LORA, A_KV_LORA, A_ROPE,
            S_WIDTH,
            C_HEADS * C_HEAD_DIM, C_HEAD_DIM, C_HEAD_DIM,
            I_HEADS * I_DIM, I_DIM, I_HEADS)
IN_WIDTH = sum(IN_SIZES)

kernel_name = "hybrid_mla_s5_dsa_deepnorm_trunk"


def layer_norm(x, g, b):
    xf = x.astype(jnp.float32)
    mu = jnp.mean(xf, axis=-1, keepdims=True)
    var = jnp.mean(jnp.square(xf - mu), axis=-1, keepdims=True)
    y = (xf - mu) * lax.rsqrt(var + LN_EPS) * g.astype(jnp.float32) + b.astype(jnp.float32)
    return y.astype(x.dtype)


def rms_norm(x, g):
    xf = x.astype(jnp.float32)
    y = xf * lax.rsqrt(jnp.mean(xf * xf, axis=-1, keepdims=True) + RMS_EPS) * g.astype(jnp.float32)
    return y.astype(x.dtype)


def rope_tables(pos, dim):
    inv = ROPE_THETA ** (-jnp.arange(0, dim, 2, dtype=jnp.float32) / dim)
    ang = pos.astype(jnp.float32)[:, None] * inv[None, :]
    return jnp.cos(ang), jnp.sin(ang)


def apply_rope(x, cos, sin):
    half = x.shape[-1] // 2
    x1 = x[..., :half].astype(jnp.float32)
    x2 = x[..., half:].astype(jnp.float32)
    return jnp.concatenate([x1 * cos - x2 * sin, x2 * cos + x1 * sin], axis=-1).astype(x.dtype)


def t5_bucket(rel):
    half = T5_BUCKETS // 2
    max_exact = half // 2
    ret = jnp.where(rel > 0, half, 0)
    n = jnp.abs(rel)
    nf = jnp.maximum(n, 1).astype(jnp.float32)
    large = max_exact + (jnp.log(nf / max_exact) / math.log(T5_MAX_DIST / max_exact)
                         * (half - max_exact)).astype(jnp.int32)
    large = jnp.minimum(large, half - 1)
    return ret + jnp.where(n < max_exact, n, large)


def to_blocks(a):
    bsz, length = a.shape[:2]
    return jnp.moveaxis(a.reshape((bsz, length // Q_BLOCK, Q_BLOCK) + a.shape[2:]), 1, 0)


def from_blocks(a):
    n, bsz, qb = a.shape[:3]
    return jnp.moveaxis(a, 0, 1).reshape((bsz, n * qb) + a.shape[3:])


def mla_mixer(q_lat, kv_lat, k_rope_raw, g_q, w_uq, g_kv, w_ukv, cos, sin, chunk):
    bsz, length, _ = q_lat.shape
    q = (rms_norm(q_lat, g_q) @ w_uq).reshape(bsz, length, A_HEADS, A_NOPE + A_ROPE)
    q_nope = q[..., :A_NOPE]
    q_rope = apply_rope(q[..., A_NOPE:], cos[:, None], sin[:, None])
    kv = (rms_norm(kv_lat, g_kv) @ w_ukv).reshape(bsz, length, A_HEADS, A_NOPE + A_VDIM)
    k_nope, v = kv[..., :A_NOPE], kv[..., A_NOPE:]
    k_rope = apply_rope(k_rope_raw, cos, sin)
    scale = (A_NOPE + A_ROPE) ** -0.5

    def attend(args):
        qn, qr, cq = args
        s = jnp.einsum('bqhd,bkhd->bhqk', qn, k_nope) + jnp.einsum('bqhd,bkd->bhqk', qr, k_rope)
        s = s.astype(jnp.float32) * scale
        s = jnp.where((chunk[None, :] <= cq[:, None])[None, None], s, -jnp.inf)
        p = jax.nn.softmax(s, axis=-1).astype(v.dtype)
        return jnp.einsum('bhqk,bkhd->bqhd', p, v)

    o = lax.map(attend, (to_blocks(q_nope), to_blocks(q_rope), chunk.reshape(-1, Q_BLOCK)))
    return from_blocks(o).reshape(bsz, length, A_HEADS * A_VDIM)


def _complex_linear_combine(e1, e2):
    a1r, a1i, b1r, b1i = e1
    a2r, a2i, b2r, b2i = e2
    ar = a2r * a1r - a2i * a1i
    ai = a2r * a1i + a2i * a1r
    br = a2r * b1r - a2i * b1i + b2r
    bi = a2r * b1i + a2i * b1r + b2i
    return ar, ai, br, bi


def s5_mixer(u, lam_re, lam_im, log_step, b_re, b_im, c_re, c_im, d_skip, w_glu, b_glu):
    f32 = jnp.float32
    bsz, length, _ = u.shape
    uf = u.astype(f32).reshape(bsz, length, S_GROUPS, S_GROUP_CH)
    lr, li = lam_re.astype(f32), lam_im.astype(f32)
    dt = jnp.exp(log_step.astype(f32))[:, None]
    mag = jnp.exp(lr * dt)
    ab_re, ab_im = mag * jnp.cos(li * dt), mag * jnp.sin(li * dt)
    den = lr * lr + li * li
    nr, ni = ab_re - 1.0, ab_im
    co_re = (nr * lr + ni * li) / den
    co_im = (ni * lr - nr * li) / den
    br, bi = b_re.astype(f32), b_im.astype(f32)
    bb_re = co_re[..., None] * br - co_im[..., None] * bi
    bb_im = co_re[..., None] * bi + co_im[..., None] * br
    bu_re = jnp.einsum('blgp,gnp->blgn', uf, bb_re)
    bu_im = jnp.einsum('blgp,gnp->blgn', uf, bb_im)
    a_re = jnp.broadcast_to(ab_re, (1, length, S_GROUPS, S_STATE))
    a_im = jnp.broadcast_to(ab_im, (1, length, S_GROUPS, S_STATE))
    _, _, h_re, h_im = lax.associative_scan(_complex_linear_combine, (a_re, a_im, bu_re, bu_im), axis=1)
    y = (jnp.einsum('blgn,gpn->blgp', h_re, c_re.astype(f32))
         - jnp.einsum('blgn,gpn->blgp', h_im, c_im.astype(f32))
         + d_skip.astype(f32) * uf)
    g = jax.nn.gelu(y).reshape(bsz, length, S_WIDTH)
    out = g * jax.nn.sigmoid(g @ w_glu.astype(f32) + b_glu.astype(f32))
    return out.astype(u.dtype)


def dsa_mixer(q, k, v, q_i, k_i, w_i, rel_bias, cos, sin, pos, k_sel):
    f32 = jnp.float32
    bsz, length, _ = q.shape
    q = q.reshape(bsz, length, C_HEADS, C_HEAD_DIM)
    q_i = q_i.reshape(bsz, length, I_HEADS, I_DIM)
    q_i = jnp.concatenate([apply_rope(q_i[..., :I_ROPE], cos[:, None], sin[:, None]), q_i[..., I_ROPE:]], axis=-1)
    k_i = jnp.concatenate([apply_rope(k_i[..., :I_ROPE], cos, sin), k_i[..., I_ROPE:]], axis=-1)
    chunk = pos // CHUNK
    scale = C_HEAD_DIM ** -0.5
    idx_scale = (I_DIM * I_HEADS) ** -0.5
    gather = jax.vmap(lambda a, i: a[i])

    def attend(args):
        qb, qib, wib, tq = args
        cq = tq // CHUNK
        dots = jnp.einsum('bqhd,bkd->bqhk', qib, k_i).astype(f32)
        score = jnp.einsum('bqhk,bqh->bqk', jax.nn.relu(dots), wib.astype(f32)) * idx_scale
        admissible = chunk[None, :] <= cq[:, None]
        score = jnp.where(admissible[None], score, -jnp.inf)
        _, sel = lax.top_k(score, k_sel)
        valid = (sel // CHUNK) <= cq[None, :, None]
        kg = gather(k, sel)
        vg = gather(v, sel)
        bias = rel_bias[t5_bucket(sel - tq[None, :, None])]
        s = (jnp.einsum('bqhd,bqkd->bhqk', qb, kg).astype(f32) * scale
             + jnp.moveaxis(bias, -1, 1).astype(f32))
        s = jnp.where(valid[:, None], s, -jnp.inf)
        p = jax.nn.softmax(s, axis=-1).astype(vg.dtype)
        return jnp.einsum('bhqk,bqkd->bqhd', p, vg)

    o = lax.map(attend, (to_blocks(q), to_blocks(q_i), to_blocks(w_i), pos.reshape(-1, Q_BLOCK)))
    return from_blocks(o).reshape(bsz, length, C_HEADS * C_HEAD_DIM)


def setup_inputs(seed: int = 0) -> dict:
    key = jax.random.key(seed)
    ks = jax.random.split(key, 26)
    f32 = jnp.float32

    def nrm(k, shape, s):
        return jax.random.normal(k, shape, f32) * s

    x = nrm(ks[0], (BATCH, SEQ, D_MODEL), 1.0)
    ln_in_g = 1.0 + nrm(ks[1], (D_MODEL,), 0.01)
    ln_in_b = nrm(ks[2], (D_MODEL,), 0.01)
    rel_bias = nrm(ks[3], (T5_BUCKETS, C_HEADS), 0.2)
    w_in = nrm(ks[4], (DEPTH, D_MODEL, IN_WIDTH), D_MODEL ** -0.5)
    a_gq = 1.0 + nrm(ks[5], (DEPTH, A_Q_LORA), 0.01)
    a_wuq = nrm(ks[6], (DEPTH, A_Q_LORA, A_HEADS * (A_NOPE + A_ROPE)), A_Q_LORA ** -0.5)
    a_gkv = 1.0 + nrm(ks[7], (DEPTH, A_KV_LORA), 0.01)
    a_wukv = nrm(ks[8], (DEPTH, A_KV_LORA, A_HEADS * (A_NOPE + A_VDIM)), A_KV_LORA ** -0.5)
    s_lam_re = -0.5 + nrm(ks[9], (DEPTH, S_GROUPS, S_STATE), 0.01)
    s_lam_im = jnp.broadcast_to(math.pi * jnp.arange(S_STATE, dtype=f32), (DEPTH, S_GROUPS, S_STATE))
    s_log_step = jax.random.uniform(ks[10], (DEPTH, S_GROUPS), f32,
                                    minval=math.log(S_DT_MIN), maxval=math.log(S_DT_MAX))
    s_b_re = nrm(ks[11], (DEPTH, S_GROUPS, S_STATE, S_GROUP_CH), (2 * S_GROUP_CH) ** -0.5)
    s_b_im = nrm(ks[12], (DEPTH, S_GROUPS, S_STATE, S_GROUP_CH), (2 * S_GROUP_CH) ** -0.5)
    s_c_re = nrm(ks[13], (DEPTH, S_GROUPS, S_GROUP_CH, S_STATE), (2 * S_STATE) ** -0.5)
    s_c_im = nrm(ks[14], (DEPTH, S_GROUPS, S_GROUP_CH, S_STATE), (2 * S_STATE) ** -0.5)
    s_d = nrm(ks[15], (DEPTH, S_GROUPS, S_GROUP_CH), 1.0)
    s_w_glu = nrm(ks[16], (DEPTH, S_WIDTH, S_WIDTH), S_WIDTH ** -0.5)
    s_b_glu = nrm(ks[17], (DEPTH, S_WIDTH), 0.01)
    w_out = nrm(ks[18], (DEPTH, MIX_WIDTH, D_MODEL), MIX_WIDTH ** -0.5 * DEEPNORM_BETA)
    ln1_g = 1.0 + nrm(ks[19], (DEPTH, D_MODEL), 0.01)
    ln1_b = nrm(ks[20], (DEPTH, D_MODEL), 0.01)
    w_ff1 = nrm(ks[21], (DEPTH, D_MODEL, D_FF), D_MODEL ** -0.5)
    w_ff2 = nrm(ks[22], (DEPTH, D_FF, D_MODEL), D_FF ** -0.5 * DEEPNORM_BETA)
    ln2_g = 1.0 + nrm(ks[23], (DEPTH, D_MODEL), 0.01)
    ln2_b = nrm(ks[24], (DEPTH, D_MODEL), 0.01)
    return {"x": x, "ln_in_g": ln_in_g, "ln_in_b": ln_in_b, "rel_bias": rel_bias, "w_in": w_in,
            "a_gq": a_gq, "a_wuq": a_wuq, "a_gkv": a_gkv, "a_wukv": a_wukv,
            "s_lam_re": s_lam_re, "s_lam_im": s_lam_im, "s_log_step": s_log_step,
            "s_b_re": s_b_re, "s_b_im": s_b_im, "s_c_re": s_c_re, "s_c_im": s_c_im, "s_d": s_d,
            "s_w_glu": s_w_glu, "s_b_glu": s_b_glu, "w_out": w_out, "ln1_g": ln1_g, "ln1_b": ln1_b,
            "w_ff1": w_ff1, "w_ff2": w_ff2, "ln2_g": ln2_g, "ln2_b": ln2_b}


def reference(x, ln_in_g, ln_in_b, rel_bias, w_in, a_gq, a_wuq, a_gkv, a_wukv,
              s_lam_re, s_lam_im, s_log_step, s_b_re, s_b_im, s_c_re, s_c_im, s_d,
              s_w_glu, s_b_glu, w_out, ln1_g, ln1_b, w_ff1, w_ff2, ln2_g, ln2_b):
    bsz, length, _ = x.shape
    pos = jnp.arange(length, dtype=jnp.int32)
    chunk = pos // CHUNK
    cos_a, sin_a = rope_tables(pos, A_ROPE)
    cos_i, sin_i = rope_tables(pos, I_ROPE)
    k_sel = min(TOPK_MAX, length // 4)
    splits = np.cumsum(IN_SIZES)[:-1].tolist()

    h = layer_norm(x, ln_in_g, ln_in_b)
    for l in range(DEPTH):
        proj = h @ w_in[l]
        q_lat, kv_lat, k_rope, u_ssm, q_c, k_c, v_c, q_i, k_i, w_i = jnp.split(proj, splits, axis=-1)
        a_out = mla_mixer(q_lat, kv_lat, k_rope, a_gq[l], a_wuq[l], a_gkv[l], a_wukv[l],
                          cos_a, sin_a, chunk)
        b_out = s5_mixer(u_ssm, s_lam_re[l], s_lam_im[l], s_log_step[l], s_b_re[l], s_b_im[l],
                         s_c_re[l], s_c_im[l], s_d[l], s_w_glu[l], s_b_glu[l])
        c_out = dsa_mixer(q_c, k_c, v_c, q_i, k_i, w_i, rel_bias, cos_i, sin_i, pos, k_sel)
        mix = jnp.concatenate([a_out, b_out, c_out], axis=-1) @ w_out[l]
        h = layer_norm(DEEPNORM_ALPHA * h + mix, ln1_g[l], ln1_b[l])
        ff = jnp.square(jax.nn.relu(h @ w_ff1[l])) @ w_ff2[l]
        h = layer_norm(DEEPNORM_ALPHA * h + ff, ln2_g[l], ln2_b[l])
    return h
```

```python
import functools
import math

import jax
import jax.numpy as jnp
from jax import lax
from jax.experimental import pallas as pl
from jax.experimental.pallas import tpu as pltpu

CHUNK = 64
A_HEADS, A_NOPE, A_ROPE, A_VDIM = 12, 128, 64, 128
A_Q_LORA, A_KV_LORA = 768, 512
S_GROUP_CH, S_WIDTH, S_STATE = 16, 1024, 64
S_GROUPS = S_WIDTH // S_GROUP_CH
C_HEADS, C_HEAD_DIM = 12, 128
I_HEADS, I_DIM, I_ROPE = 16, 64, 32
TOPK_MAX = 256
T5_BUCKETS, T5_MAX_DIST = 32, 128
ROPE_THETA = 10000.0
LN_EPS = 1e-5
RMS_EPS = 1e-6

LANES = 128
SUBLANES = 8
VMEM_LIMIT_BYTES = 56 * 1024 * 1024

MM_TM, MM_TN, MM_TK = 1024, 1024, 4096
ROW_TILE = 256
PROJ_TM = 512
ATT_T = 512
DSA_T = 256
S_TC = 16
S_PAIR = 2

COL_QC, COL_QLAT, COL_KC, COL_VC, COL_KROPE, COL_KI, COL_WI = 0, 1536, 2304, 2432, 2560, 2688, 2816
COL_U, COL_QI, COL_KVLAT, PROJ_WIDTH = 3072, 4096, 5120, 5632

NEG_BIG = -1e30
INT_MIN = -2 ** 31

_bf16 = jnp.bfloat16
_f32 = jnp.float32


def _tile(n, pref):
    t = min(pref, n)
    while n % t:
        t //= 2
    return t


def _params(*sem):
    return pltpu.CompilerParams(dimension_semantics=sem, vmem_limit_bytes=VMEM_LIMIT_BYTES)


def _mm_kernel(x_ref, w_ref, o_ref, *, act):
    y = jnp.dot(x_ref[...], w_ref[...], preferred_element_type=_f32)
    if act == "relu2":
        y = jnp.square(jnp.maximum(y, 0.0))
    o_ref[...] = y.astype(o_ref.dtype)


def matmul(x, w, out_dtype, act=None):
    m, k = x.shape
    n = w.shape[1]
    tm, tn = _tile(m, MM_TM), _tile(n, MM_TN)
    return pl.pallas_call(
        functools.partial(_mm_kernel, act=act),
        grid=(m // tm, n // tn),
        in_specs=[pl.BlockSpec((tm, k), lambda i, j: (i, 0)),
                  pl.BlockSpec((k, tn), lambda i, j: (0, j))],
        out_specs=pl.BlockSpec((tm, tn), lambda i, j: (i, j)),
        out_shape=jax.ShapeDtypeStruct((m, n), out_dtype),
        compiler_params=_params("parallel", "parallel"),
        name="matmul",
    )(x, w)


def _mm_acc_kernel(x_ref, w_ref, o_ref):
    y = jnp.dot(x_ref[...], w_ref[...], preferred_element_type=_f32)

    @pl.when(pl.program_id(2) == 0)
    def _():
        o_ref[...] = y

    @pl.when(pl.program_id(2) != 0)
    def _():
        o_ref[...] += y


def matmul_ksplit(x, w):
    m, k = x.shape
    n = w.shape[1]
    tm, tn, tk = min(MM_TM, m), min(MM_TN, n), min(MM_TK, k)
    return pl.pallas_call(
        _mm_acc_kernel,
        grid=(m // tm, n // tn, k // tk),
        in_specs=[pl.BlockSpec((tm, tk), lambda i, j, l: (i, l)),
                  pl.BlockSpec((tk, tn), lambda i, j, l: (l, j))],
        out_specs=pl.BlockSpec((tm, tn), lambda i, j, l: (i, j)),
        out_shape=jax.ShapeDtypeStruct((m, n), _f32),
        compiler_params=_params("parallel", "parallel", "arbitrary"),
        name="matmul_ksplit",
    )(x, w)


def _mm3_kernel(a_ref, b_ref, c_ref, wa_ref, wb_ref, wc_ref, o_ref):
    y = jnp.dot(a_ref[...], wa_ref[...], preferred_element_type=_f32)
    y += jnp.dot(b_ref[...], wb_ref[...], preferred_element_type=_f32)
    y += jnp.dot(c_ref[...], wc_ref[...], preferred_element_type=_f32)
    o_ref[...] = y


def matmul_concat3(a, b, c, wa, wb, wc):
    m = a.shape[0]
    n = wa.shape[1]
    tm, tn = _tile(m, MM_TM), _tile(n, MM_TN)
    row = lambda i, j: (i, 0)
    col = lambda i, j: (0, j)
    return pl.pallas_call(
        _mm3_kernel,
        grid=(m // tm, n // tn),
        in_specs=[pl.BlockSpec((tm, a.shape[1]), row), pl.BlockSpec((tm, b.shape[1]), row),
                  pl.BlockSpec((tm, c.shape[1]), row),
                  pl.BlockSpec((wa.shape[0], tn), col), pl.BlockSpec((wb.shape[0], tn), col),
                  pl.BlockSpec((wc.shape[0], tn), col)],
        out_specs=pl.BlockSpec((tm, tn), lambda i, j: (i, j)),
        out_shape=jax.ShapeDtypeStruct((m, n), _f32),
        compiler_params=_params("parallel", "parallel"),
        name="matmul_concat3",
    )(a, b, c, wa, wb, wc)


def _ln_rows(z, g, b):
    mu = jnp.mean(z, axis=-1, keepdims=True)
    zc = z - mu
    var = jnp.mean(zc * zc, axis=-1, keepdims=True)
    return zc * lax.rsqrt(var + LN_EPS) * g + b


def _ln_kernel(x_ref, g_ref, b_ref, o_ref, ob_ref):
    y = _ln_rows(x_ref[...], g_ref[...], b_ref[...])
    o_ref[...] = y
    ob_ref[...] = y.astype(_bf16)


def _res_ln_kernel(h_ref, y_ref, g_ref, b_ref, o_ref, ob_ref, *, alpha):
    y = _ln_rows(alpha * h_ref[...] + y_ref[...], g_ref[...], b_ref[...])
    o_ref[...] = y
    ob_ref[...] = y.astype(_bf16)


def layer_norm(x, g, b, resid=None, alpha=1.0):
    m, d = x.shape
    tr = min(ROW_TILE, m)
    rows = pl.BlockSpec((tr, d), lambda i: (i, 0))
    vec = pl.BlockSpec((1, d), lambda i: (0, 0))
    out_shape = (jax.ShapeDtypeStruct((m, d), _f32), jax.ShapeDtypeStruct((m, d), _bf16))
    g2, b2 = g.reshape(1, d), b.reshape(1, d)
    if resid is None:
        return pl.pallas_call(_ln_kernel, grid=(m // tr,), in_specs=[rows, vec, vec],
                              out_specs=(rows, rows), out_shape=out_shape,
                              compiler_params=_params("parallel"), name="layer_norm")(x, g2, b2)
    return pl.pallas_call(functools.partial(_res_ln_kernel, alpha=alpha), grid=(m // tr,),
                          in_specs=[rows, rows, vec, vec], out_specs=(rows, rows),
                          out_shape=out_shape, compiler_params=_params("parallel"),
                          name="residual_layer_norm")(resid, x, g2, b2)


def _rope_cos_sin(length, dim):
    inv = ROPE_THETA ** (-jnp.arange(0, dim, 2, dtype=_f32) / dim)
    ang = jnp.arange(length, dtype=_f32)[:, None] * inv[None, :]
    return jnp.cos(ang), jnp.sin(ang)


def _rope_tables(length, dim, lead, width):
    cos, sin = _rope_cos_sin(length, dim)
    half = dim // 2
    zeros = lambda n: jnp.zeros((length, n), _f32)
    tail = width - lead - dim
    c = jnp.concatenate([jnp.ones((length, lead), _f32), cos, cos, zeros(tail)], axis=1)
    s_up = jnp.concatenate([zeros(lead + half), sin, zeros(tail)], axis=1)
    s_dn = jnp.concatenate([zeros(lead), -sin, zeros(half + tail)], axis=1)
    return c, s_up, s_dn


def _apply_rope(x, c, s_up, s_dn, half):
    width = x.shape[-1]
    return x * c + pltpu.roll(x, half, 1) * s_up + pltpu.roll(x, width - half, 1) * s_dn


def _rms_rows(x, g):
    return x * lax.rsqrt(jnp.mean(x * x, axis=-1, keepdims=True) + RMS_EPS) * g


A_QW = 256


def _mla_q_kernel(x_ref, g_ref, w_ref, c_ref, su_ref, sd_ref, o_ref, xn_ref):
    @pl.when(pl.program_id(1) == 0)
    def _():
        xn_ref[...] = _rms_rows(x_ref[...], g_ref[...]).astype(_bf16)

    y = jnp.dot(xn_ref[...], w_ref[...], preferred_element_type=_f32)
    y = _apply_rope(y, c_ref[...], su_ref[...], sd_ref[...], A_ROPE // 2)
    o_ref[0, 0] = y.astype(_bf16)


def _mla_kv_kernel(x_ref, kr_ref, g_ref, w_ref, c_ref, su_ref, sd_ref, k_ref, v_ref, xn_ref):
    @pl.when(pl.program_id(1) == 0)
    def _():
        xn_ref[...] = _rms_rows(x_ref[...], g_ref[...]).astype(_bf16)

    y = jnp.dot(xn_ref[...], w_ref[...], preferred_element_type=_f32)
    kr = _apply_rope(kr_ref[...], c_ref[...], su_ref[...], sd_ref[...], A_ROPE // 2)
    k_ref[0, 0] = jnp.concatenate([y[:, :A_NOPE], kr], axis=1).astype(_bf16)
    v_ref[0, 0] = y[:, A_NOPE:].astype(_bf16)


def _mla_flash_kernel(q_ref, k_ref, v_ref, o_ref, m_ref, l_ref, acc_ref, *, scale, tile):
    i = pl.program_id(2)
    q = q_ref[0, 0]
    m_ref[...] = jnp.full(m_ref.shape, NEG_BIG, _f32)
    l_ref[...] = jnp.zeros(l_ref.shape, _f32)
    acc_ref[...] = jnp.zeros(acc_ref.shape, _f32)

    def update(j, mask):
        start = pl.multiple_of(j * tile, tile)
        k = k_ref[0, 0, pl.ds(start, tile), :]
        v = v_ref[0, 0, pl.ds(start, tile), :]
        s = lax.dot_general(q, k, (((1,), (1,)), ((), ())), preferred_element_type=_f32) * scale
        if mask is not None:
            s = jnp.where(mask, s, NEG_BIG)
        m_prev = m_ref[...]
        m_new = jnp.maximum(m_prev, jnp.max(s, axis=-1, keepdims=True))
        p = jnp.exp(s - m_new)
        a = jnp.exp(m_prev - m_new)
        l_ref[...] = a * l_ref[...] + jnp.sum(p, axis=-1, keepdims=True)
        acc_ref[...] = a * acc_ref[...] + jnp.dot(p.astype(_bf16), v, preferred_element_type=_f32)
        m_ref[...] = m_new

    def body(j, carry):
        update(j, None)
        return carry

    lax.fori_loop(0, i, body, 0)
    row = lax.broadcasted_iota(jnp.int32, (tile, tile), 0) // CHUNK
    col = lax.broadcasted_iota(jnp.int32, (tile, tile), 1) // CHUNK
    update(i, col <= row)
    o_ref[...] = (acc_ref[...] / l_ref[...]).astype(o_ref.dtype)


def mla_mixer(proj, bsz, length, g_q, w_uq, g_kv, w_ukv):
    m = bsz * length
    tm = min(PROJ_TM, length)
    nt = length // tm
    heads = A_HEADS
    qd = A_NOPE + A_ROPE
    wq = jnp.pad(w_uq.reshape(A_Q_LORA, heads, qd), ((0, 0), (0, 0), (0, A_QW - qd)))
    wq = wq.reshape(A_Q_LORA, heads * A_QW).astype(_bf16)
    wkv = w_ukv.astype(_bf16)
    qc, qsu, qsd = _rope_tables(length, A_ROPE, A_NOPE, A_QW)
    kc, ksu, ksd = _rope_tables(length, A_ROPE, 0, LANES)

    tab_q = pl.BlockSpec((tm, A_QW), lambda i, h: (i % nt, 0))
    tab_k = pl.BlockSpec((tm, LANES), lambda i, h: (i % nt, 0))
    head_out = lambda w: pl.BlockSpec((1, 1, tm, w), lambda i, h: (i // nt, h, i % nt, 0))

    q = pl.pallas_call(
        _mla_q_kernel,
        grid=(m // tm, heads),
        in_specs=[pl.BlockSpec((tm, A_Q_LORA), lambda i, h: (i, COL_QLAT // A_Q_LORA)),
                  pl.BlockSpec((1, A_Q_LORA), lambda i, h: (0, 0)),
                  pl.BlockSpec((A_Q_LORA, A_QW), lambda i, h: (0, h)),
                  tab_q, tab_q, tab_q],
        out_specs=head_out(A_QW),
        out_shape=jax.ShapeDtypeStruct((bsz, heads, length, A_QW), _bf16),
        scratch_shapes=[pltpu.VMEM((tm, A_Q_LORA), _bf16)],
        compiler_params=_params("parallel", "arbitrary"),
        name="mla_q_proj",
    )(proj, g_q.reshape(1, -1), wq, qc, qsu, qsd)

    k, v = pl.pallas_call(
        _mla_kv_kernel,
        grid=(m // tm, heads),
        in_specs=[pl.BlockSpec((tm, A_KV_LORA), lambda i, h: (i, COL_KVLAT // A_KV_LORA)),
                  pl.BlockSpec((tm, LANES), lambda i, h: (i, COL_KROPE // LANES)),
                  pl.BlockSpec((1, A_KV_LORA), lambda i, h: (0, 0)),
                  pl.BlockSpec((A_KV_LORA, A_NOPE + A_VDIM), lambda i, h: (0, h)),
                  tab_k, tab_k, tab_k],
        out_specs=(head_out(A_QW), head_out(A_VDIM)),
        out_shape=(jax.ShapeDtypeStruct((bsz, heads, length, A_QW), _bf16),
                   jax.ShapeDtypeStruct((bsz, heads, length, A_VDIM), _bf16)),
        scratch_shapes=[pltpu.VMEM((tm, A_KV_LORA), _bf16)],
        compiler_params=_params("parallel", "arbitrary"),
        name="mla_kv_proj",
    )(proj, proj, g_kv.reshape(1, -1), wkv, kc, ksu, ksd)

    t = min(ATT_T, length)
    nq = length // t
    return pl.pallas_call(
        functools.partial(_mla_flash_kernel, scale=float(qd) ** -0.5, tile=t),
        grid=(bsz, heads, nq),
        in_specs=[pl.BlockSpec((1, 1, t, A_QW), lambda b, h, i: (b, h, i, 0)),
                  pl.BlockSpec((1, 1, length, A_QW), lambda b, h, i: (b, h, 0, 0)),
                  pl.BlockSpec((1, 1, length, A_VDIM), lambda b, h, i: (b, h, 0, 0))],
        out_specs=pl.BlockSpec((t, A_VDIM), lambda b, h, i: (b * nq + i, h)),
        out_shape=jax.ShapeDtypeStruct((m, heads * A_VDIM), _bf16),
        scratch_shapes=[pltpu.VMEM((t, 1), _f32), pltpu.VMEM((t, 1), _f32),
                        pltpu.VMEM((t, A_VDIM), _f32)],
        compiler_params=_params("parallel", "parallel", "arbitrary"),
        name="mla_flash",
    )(q, k, v)


def _s5_operators(lam_re, lam_im, log_step, b_re, b_im, c_re, c_im, d_skip):
    hp = lax.Precision.HIGHEST
    g, n, p, tc = S_GROUPS, S_STATE, S_GROUP_CH, S_TC
    lr, li = lam_re.astype(_f32), lam_im.astype(_f32)
    dt = jnp.exp(log_step.astype(_f32))[:, None]
    steps = jnp.arange(tc + 1, dtype=_f32)[:, None, None]
    mag = jnp.exp(lr * dt * steps)
    pw_re, pw_im = mag * jnp.cos(li * dt * steps), mag * jnp.sin(li * dt * steps)
    ab_re, ab_im = pw_re[1], pw_im[1]
    den = lr * lr + li * li
    nr, ni = ab_re - 1.0, ab_im
    co_re = (nr * lr + ni * li) / den
    co_im = (ni * lr - nr * li) / den
    br, bi = b_re.astype(_f32), b_im.astype(_f32)
    bb_re = co_re[..., None] * br - co_im[..., None] * bi
    bb_im = co_re[..., None] * bi + co_im[..., None] * br
    cr, ci = c_re.astype(_f32), c_im.astype(_f32)
    cp_re = cr[None] * pw_re[:, :, None, :] - ci[None] * pw_im[:, :, None, :]
    cp_im = cr[None] * pw_im[:, :, None, :] + ci[None] * pw_re[:, :, None, :]
    kern = (jnp.einsum("dgpn,gnq->dgpq", cp_re, bb_re, precision=hp)
            - jnp.einsum("dgpn,gnq->dgpq", cp_im, bb_im, precision=hp))
    s_idx = jnp.arange(tc)[:, None]
    t_idx = jnp.arange(tc)[None, :]
    lag = t_idx - s_idx
    toe = kern[jnp.clip(lag, 0, tc)]
    toe = jnp.where((lag >= 0)[:, :, None, None, None], toe, 0.0)
    eye = (lag == 0)[:, :, None, None, None] * jnp.eye(p, dtype=_f32)[None, None, None]
    toe = toe + eye * d_skip.astype(_f32)[None, None, :, :, None]
    mg = jnp.transpose(toe, (2, 0, 4, 1, 3)).reshape(g, tc * p, tc * p)
    pr, pi_ = pw_re[tc - 1 - jnp.arange(tc)], pw_im[tc - 1 - jnp.arange(tc)]
    bs_re = pr[..., None] * bb_re[None] - pi_[..., None] * bb_im[None]
    bs_im = pr[..., None] * bb_im[None] + pi_[..., None] * bb_re[None]
    bs_re = jnp.transpose(bs_re, (1, 0, 3, 2)).reshape(g, tc * p, n)
    bs_im = jnp.transpose(bs_im, (1, 0, 3, 2)).reshape(g, tc * p, n)
    cs_re = jnp.transpose(cp_re[1:], (1, 3, 0, 2)).reshape(g, n, tc * p)
    cs_im = -jnp.transpose(cp_im[1:], (1, 3, 0, 2)).reshape(g, n, tc * p)

    def blockdiag(a):
        a = a.reshape(g // S_PAIR, S_PAIR, a.shape[1], a.shape[2])
        z = jnp.zeros_like(a[:, 0])
        top = jnp.concatenate([a[:, 0], z], axis=2)
        bot = jnp.concatenate([z, a[:, 1]], axis=2)
        return jnp.concatenate([top, bot], axis=1)

    mp = blockdiag(mg).astype(_bf16)
    bp = jnp.concatenate([blockdiag(bs_re), blockdiag(bs_im)], axis=2).astype(_bf16)
    cp = jnp.concatenate([blockdiag(cs_re), blockdiag(cs_im)], axis=1).astype(_bf16)
    ap = jnp.stack([pw_re[tc].reshape(g // S_PAIR, S_PAIR * n),
                    pw_im[tc].reshape(g // S_PAIR, S_PAIR * n)], axis=1)
    return mp, bp, cp, ap


def _gelu_tanh(x):
    return 0.5 * x * (1.0 + jnp.tanh(math.sqrt(2.0 / math.pi) * (x + 0.044715 * (x * x * x))))


def _s5_kernel(u_ref, mp_ref, bp_ref, cp_ref, ap_ref, o_ref, s_ref, hp_ref, *, nchunks):
    u = u_ref[0, 0]
    s_ref[...] = jnp.dot(u, bp_ref[0], preferred_element_type=_f32)
    a_re = ap_ref[0, 0:1, :]
    a_im = ap_ref[0, 1:2, :]

    def step(c8, carry):
        h_re, h_im = carry
        base = pl.multiple_of(c8 * SUBLANES, SUBLANES)
        s_blk = s_ref[pl.ds(base, SUBLANES), :]
        rows_re, rows_im = [], []
        for r in range(SUBLANES):
            rows_re.append(h_re)
            rows_im.append(h_im)
            s_re = s_blk[r:r + 1, 0:LANES]
            s_im = s_blk[r:r + 1, LANES:2 * LANES]
            h_re, h_im = a_re * h_re - a_im * h_im + s_re, a_re * h_im + a_im * h_re + s_im
        hp_ref[pl.ds(base, SUBLANES), :] = jnp.concatenate(
            [jnp.concatenate(rows_re, axis=0), jnp.concatenate(rows_im, axis=0)], axis=1)
        return h_re, h_im

    zero = jnp.zeros((1, LANES), _f32)
    lax.fori_loop(0, nchunks // SUBLANES, step, (zero, zero))
    y = jnp.dot(u, mp_ref[0], preferred_element_type=_f32)
    y += jnp.dot(hp_ref[...].astype(_bf16), cp_ref[0], preferred_element_type=_f32)
    o_ref[0, 0] = _gelu_tanh(y)


def _glu_kernel(g_ref, w_ref, b_ref, o_ref):
    g = g_ref[...]
    z = jnp.dot(g.astype(_bf16), w_ref[...], preferred_element_type=_f32) + b_ref[...]
    o_ref[...] = (g * jax.nn.sigmoid(z)).astype(o_ref.dtype)


def s5_mixer(proj, bsz, length, lam_re, lam_im, log_step, b_re, b_im, c_re, c_im, d_skip, w_glu, b_glu):
    m = bsz * length
    tc, p = S_TC, S_GROUP_CH
    npair = S_GROUPS // S_PAIR
    nch = length // tc
    wide = S_PAIR * tc * p
    mp, bp, cp, ap = _s5_operators(lam_re, lam_im, log_step, b_re, b_im, c_re, c_im, d_skip)
    u = proj[:, COL_U:COL_U + S_WIDTH].reshape(bsz, nch, tc, npair, S_PAIR, p)
    u = jnp.transpose(u, (0, 3, 1, 4, 2, 5)).reshape(bsz, npair, nch, wide).astype(_bf16)
    g = pl.pallas_call(
        functools.partial(_s5_kernel, nchunks=nch),
        grid=(npair, bsz),
        in_specs=[pl.BlockSpec((1, 1, nch, wide), lambda gp, b: (b, gp, 0, 0)),
                  pl.BlockSpec((1, wide, wide), lambda gp, b: (gp, 0, 0)),
                  pl.BlockSpec((1, wide, 2 * LANES), lambda gp, b: (gp, 0, 0)),
                  pl.BlockSpec((1, 2 * LANES, wide), lambda gp, b: (gp, 0, 0)),
                  pl.BlockSpec((1, 2, LANES), lambda gp, b: (gp, 0, 0))],
        out_specs=pl.BlockSpec((1, 1, nch, wide), lambda gp, b: (b, gp, 0, 0)),
        out_shape=jax.ShapeDtypeStruct((bsz, npair, nch, wide), _f32),
        scratch_shapes=[pltpu.VMEM((nch, 2 * LANES), _f32), pltpu.VMEM((nch, 2 * LANES), _f32)],
        compiler_params=_params("parallel", "parallel"),
        name="s5_scan",
    )(u, mp, bp, cp, ap)
    g = g.reshape(bsz, npair, nch, S_PAIR, tc, p)
    g = jnp.transpose(g, (0, 2, 4, 1, 3, 5)).reshape(m, S_WIDTH)
    tm = min(MM_TM, m)
    return pl.pallas_call(
        _glu_kernel,
        grid=(m // tm,),
        in_specs=[pl.BlockSpec((tm, S_WIDTH), lambda i: (i, 0)),
                  pl.BlockSpec((S_WIDTH, S_WIDTH), lambda i: (0, 0)),
                  pl.BlockSpec((1, S_WIDTH), lambda i: (0, 0))],
        out_specs=pl.BlockSpec((tm, S_WIDTH), lambda i: (i, 0)),
        out_shape=jax.ShapeDtypeStruct((m, S_WIDTH), _bf16),
        compiler_params=_params("parallel"),
        name="s5_glu",
    )(g, w_glu.astype(_bf16), b_glu.reshape(1, -1).astype(_f32))


def _t5_bucket(rel):
    half = T5_BUCKETS // 2
    max_exact = half // 2
    ret = jnp.where(rel > 0, half, 0)
    n = jnp.abs(rel)
    nf = jnp.maximum(n, 1).astype(_f32)
    large = max_exact + (jnp.log(nf / max_exact) / math.log(T5_MAX_DIST / max_exact)
                         * (half - max_exact)).astype(jnp.int32)
    large = jnp.minimum(large, half - 1)
    return ret + jnp.where(n < max_exact, n, large)


def _dsa_bias_tiles(rel_bias, t):
    r = jnp.arange(t, dtype=jnp.int32)[:, None]
    c = jnp.arange(t, dtype=jnp.int32)[None, :]
    rel = jnp.stack([c - r - 2 * t, c - r - t, c - r])
    return jnp.transpose(rel_bias.astype(_f32)[_t5_bucket(rel)], (3, 0, 1, 2))


def _dsa_prep_kernel(qi_ref, ki_ref, qc_ref, qsu_ref, qsd_ref, kc_ref, ksu_ref, ksd_ref, qo_ref, ko_ref):
    half = I_ROPE // 2
    rep = (I_HEADS * I_DIM) // LANES
    tile = lambda r: jnp.concatenate([r[...]] * rep, axis=1)
    q = _apply_rope(qi_ref[...], tile(qc_ref), tile(qsu_ref), tile(qsd_ref), half)
    for h in range(I_HEADS):
        qo_ref[0, h] = q[:, h * I_DIM:(h + 1) * I_DIM].astype(_bf16)
    ko_ref[...] = _apply_rope(ki_ref[...], kc_ref[...], ksu_ref[...], ksd_ref[...], half).astype(_bf16)


def _float_key(x):
    b = lax.bitcast_convert_type(x, jnp.int32)
    return b ^ ((b >> 31) & 0x7FFFFFFF)


def _dsa_kernel(qi_ref, wi_ref, kit_ref, qc_ref, kct_ref, vc_ref, bias_ref, o_ref,
                keys_ref, wb_ref, qb_ref, m_ref, l_ref, acc_ref, *, tile, k_sel, scale, idx_scale):
    i = pl.program_id(1)
    half = tile // 2

    w = wi_ref[...]
    for h in range(I_HEADS):
        wb_ref[h] = jnp.broadcast_to(w[:, h:h + 1], (tile, LANES))
    qb_ref[...] = qc_ref[...].astype(_bf16)

    row_chunk = lax.broadcasted_iota(jnp.int32, (tile, tile), 0) // CHUNK
    col_chunk = lax.broadcasted_iota(jnp.int32, (tile, tile), 1) // CHUNK
    admissible_diag = col_chunk <= row_chunk

    def score_block(j, carry):
        kt = kit_ref[0, j]
        acc = jnp.zeros((tile, tile), _f32)
        for h in range(I_HEADS):
            d = jnp.dot(qi_ref[0, h], kt, preferred_element_type=_f32)
            wh = wb_ref[h]
            acc += jnp.maximum(d, 0.0) * jnp.concatenate([wh] * (tile // LANES), axis=1)
        key = _float_key(acc * idx_scale)
        keys_ref[j] = jnp.where(jnp.logical_or(j < i, admissible_diag), key, INT_MIN)
        return carry

    lax.fori_loop(0, i + 1, score_block, 0)

    def count_ge(t):
        tb = jnp.broadcast_to(t, (tile, LANES))

        def blk(j, c):
            kj = keys_ref[j]
            for s in range(tile // LANES):
                c += jnp.where(kj[:, s * LANES:(s + 1) * LANES] >= tb, 1, 0)
            return c

        c = lax.fori_loop(0, i + 1, blk, jnp.zeros((tile, LANES), jnp.int32))
        return jnp.sum(c, axis=1, keepdims=True)

    zero = jnp.zeros((tile, 1), jnp.int32)
    thr = jnp.where(count_ge(zero) >= k_sel, zero, jnp.full((tile, 1), INT_MIN, jnp.int32))

    def bit_step(b, t):
        cand = t | jnp.left_shift(jnp.int32(1), 30 - b)
        return jnp.where(count_ge(cand) >= k_sel, cand, t)

    thr = lax.fori_loop(0, 31, bit_step, thr)
    thr = jnp.maximum(thr, INT_MIN + 1)
    thr_b = jnp.broadcast_to(thr, (tile, tile))

    m_ref[...] = jnp.full(m_ref.shape, NEG_BIG, _f32)
    l_ref[...] = jnp.zeros(l_ref.shape, _f32)
    acc_ref[...] = jnp.zeros(acc_ref.shape, _f32)

    def attend_block(j, carry):
        mask = keys_ref[j] >= thr_b
        kt = kct_ref[0, j]
        v = vc_ref[0, j]
        sel = jnp.clip(j - i + 2, 0, 2)
        for h in range(C_HEADS):
            q = qb_ref[:, h * C_HEAD_DIM:(h + 1) * C_HEAD_DIM]
            s = jnp.dot(q, kt, preferred_element_type=_f32) * scale + bias_ref[h, sel]
            s = jnp.where(mask, s, NEG_BIG)
            m_prev = m_ref[h]
            m_new = jnp.maximum(m_prev, jnp.max(s, axis=-1, keepdims=True))
            p = jnp.where(mask, jnp.exp(s - m_new), 0.0)
            a = jnp.exp(m_prev - m_new)
            l_ref[h] = a * l_ref[h] + jnp.sum(p, axis=-1, keepdims=True)
            acc_ref[h] = a * acc_ref[h] + jnp.dot(p.astype(_bf16), v, preferred_element_type=_f32)
            m_ref[h] = m_new
        return carry

    lax.fori_loop(0, i + 1, attend_block, 0)
    for h in range(C_HEADS):
        o_ref[:, h * C_HEAD_DIM:(h + 1) * C_HEAD_DIM] = (acc_ref[h] / l_ref[h]).astype(o_ref.dtype)


def dsa_mixer(proj, bsz, length, rel_bias):
    m = bsz * length
    t = min(DSA_T, length)
    nb = length // t
    k_sel = min(TOPK_MAX, length // 4)
    tm = min(PROJ_TM, length)
    nt = length // tm
    iw = I_HEADS * I_DIM
    qtab = _rope_tables(length, I_ROPE, 0, I_DIM)
    qtab = [jnp.concatenate([a.at[:, I_ROPE:].set(1.0) if n == 0 else a] * (LANES // I_DIM), axis=1)
            for n, a in enumerate(qtab)]
    ktab = _rope_tables(length, I_ROPE, 0, LANES)
    ktab = [ktab[0].at[:, I_ROPE:I_DIM].set(1.0), ktab[1], ktab[2]]
    tab = pl.BlockSpec((tm, LANES), lambda i: (i % nt, 0))
    qi, ki = pl.pallas_call(
        _dsa_prep_kernel,
        grid=(m // tm,),
        in_specs=[pl.BlockSpec((tm, iw), lambda i: (i, COL_QI // iw)),
                  pl.BlockSpec((tm, LANES), lambda i: (i, COL_KI // LANES)),
                  tab, tab, tab, tab, tab, tab],
        out_specs=(pl.BlockSpec((1, I_HEADS, tm, I_DIM), lambda i: (i // nt, 0, i % nt, 0)),
                   pl.BlockSpec((tm, LANES), lambda i: (i, 0))),
        out_shape=(jax.ShapeDtypeStruct((bsz, I_HEADS, length, I_DIM), _bf16),
                   jax.ShapeDtypeStruct((m, LANES), _bf16)),
        compiler_params=_params("parallel"),
        name="dsa_prep",
    )(proj, proj, *qtab, *ktab)

    kit = jnp.transpose(ki[:, :I_DIM].reshape(bsz, nb, t, I_DIM), (0, 1, 3, 2))
    kct = jnp.transpose(proj[:, COL_KC:COL_KC + C_HEAD_DIM].astype(_bf16).reshape(bsz, nb, t, C_HEAD_DIM),
                        (0, 1, 3, 2))
    vc = proj[:, COL_VC:COL_VC + C_HEAD_DIM].astype(_bf16).reshape(bsz, nb, t, C_HEAD_DIM)
    bias = _dsa_bias_tiles(rel_bias, t)
    qw = C_HEADS * C_HEAD_DIM
    whole = lambda shape: pl.BlockSpec((1,) + shape, lambda b, i: (b, 0, 0, 0))
    return pl.pallas_call(
        functools.partial(_dsa_kernel, tile=t, k_sel=k_sel, scale=float(C_HEAD_DIM) ** -0.5,
                          idx_scale=float(I_DIM * I_HEADS) ** -0.5),
        grid=(bsz, nb),
        in_specs=[pl.BlockSpec((1, I_HEADS, t, I_DIM), lambda b, i: (b, 0, i, 0)),
                  pl.BlockSpec((t, LANES), lambda b, i: (b * nb + i, COL_WI // LANES)),
                  whole((nb, I_DIM, t)),
                  pl.BlockSpec((t, qw), lambda b, i: (b * nb + i, COL_QC // qw)),
                  whole((nb, C_HEAD_DIM, t)),
                  whole((nb, t, C_HEAD_DIM)),
                  pl.BlockSpec((C_HEADS, 3, t, t), lambda b, i: (0, 0, 0, 0),
                               pipeline_mode=pl.Buffered(1))],
        out_specs=pl.BlockSpec((t, qw), lambda b, i: (b * nb + i, 0)),
        out_shape=jax.ShapeDtypeStruct((m, qw), _bf16),
        scratch_shapes=[pltpu.VMEM((nb, t, t), jnp.int32),
                        pltpu.VMEM((I_HEADS, t, LANES), _f32),
                        pltpu.VMEM((t, qw), _bf16),
                        pltpu.VMEM((C_HEADS, t, 1), _f32),
                        pltpu.VMEM((C_HEADS, t, 1), _f32),
                        pltpu.VMEM((C_HEADS, t, C_HEAD_DIM), _f32)],
        compiler_params=_params("parallel", "arbitrary"),
        name="dsa_attention",
    )(qi, proj, kit, proj, kct, vc, bias)


def _reorder_w_in(w):
    sizes = (A_Q_LORA, A_KV_LORA, A_ROPE, S_WIDTH, C_HEADS * C_HEAD_DIM, C_HEAD_DIM, C_HEAD_DIM,
             I_HEADS * I_DIM, I_DIM, I_HEADS)
    offs = [0]
    for s in sizes:
        offs.append(offs[-1] + s)
    q_lat, kv_lat, k_rope, u, q_c, k_c, v_c, q_i, k_i, w_i = [w[:, offs[n]:offs[n + 1]] for n in range(10)]
    pad = lambda a, width: jnp.pad(a, ((0, 0), (0, width - a.shape[1])))
    cols = [q_c, q_lat, k_c, v_c, pad(k_rope, LANES), pad(k_i, LANES), pad(w_i, LANES),
            jnp.zeros((w.shape[0], LANES), w.dtype), u, q_i, kv_lat]
    out = jnp.concatenate(cols, axis=1).astype(_bf16)
    assert out.shape[1] == PROJ_WIDTH
    return out


def kernel(x, ln_in_g, ln_in_b, rel_bias, w_in, a_gq, a_wuq, a_gkv, a_wukv, s_lam_re, s_lam_im, s_log_step, s_b_re, s_b_im, s_c_re, s_c_im, s_d, s_w_glu, s_b_glu, w_out, ln1_g, ln1_b, w_ff1, w_ff2, ln2_g, ln2_b):
    bsz, length, d_model = x.shape
    depth = w_in.shape[0]
    alpha = (2 * depth) ** 0.25
    m = bsz * length
    a_w, b_w = A_HEADS * A_VDIM, S_WIDTH

    h, hb = layer_norm(x.reshape(m, d_model), ln_in_g, ln_in_b)
    for l in range(depth):
        proj = matmul(hb, _reorder_w_in(w_in[l]), _f32)
        a_out = mla_mixer(proj, bsz, length, a_gq[l], a_wuq[l], a_gkv[l], a_wukv[l])
        b_out = s5_mixer(proj, bsz, length, s_lam_re[l], s_lam_im[l], s_log_step[l], s_b_re[l], s_b_im[l],
                         s_c_re[l], s_c_im[l], s_d[l], s_w_glu[l], s_b_glu[l])
        c_out = dsa_mixer(proj, bsz, length, rel_bias)
        wo = w_out[l].astype(_bf16)
        mix = matmul_concat3(a_out, b_out, c_out, wo[:a_w], wo[a_w:a_w + b_w], wo[a_w + b_w:])
        h, hb = layer_norm(mix, ln1_g[l], ln1_b[l], resid=h, alpha=alpha)
        act = matmul(hb, w_ff1[l].astype(_bf16), _bf16, act="relu2")
        ff = matmul_ksplit(act, w_ff2[l].astype(_bf16))
        h, hb = layer_norm(ff, ln2_g[l], ln2_b[l], resid=h, alpha=alpha)
    return h.reshape(bsz, length, d_model)
```

```python
import functools
import math

import jax
import jax.numpy as jnp
from jax import lax
from jax.experimental import pallas as pl
from jax.experimental.pallas import tpu as pltpu

CHUNK = 64
A_HEADS, A_NOPE, A_ROPE, A_VDIM = 12, 128, 64, 128
A_Q_LORA, A_KV_LORA = 768, 512
S_GROUP_CH, S_WIDTH, S_STATE = 16, 1024, 64
S_GROUPS = S_WIDTH // S_GROUP_CH
C_HEADS, C_HEAD_DIM = 12, 128
I_HEADS, I_DIM, I_ROPE = 16, 64, 32
TOPK_MAX = 256
T5_BUCKETS, T5_MAX_DIST = 32, 128
ROPE_THETA = 10000.0
LN_EPS = 1e-5
RMS_EPS = 1e-6

LANES = 128
SUBLANES = 8
VMEM_LIMIT_BYTES = 56 * 1024 * 1024

MM_TM, MM_TN, MM_TK = 1024, 1024, 4096
ROW_TILE = 256
PROJ_TM = 512
ATT_T = 512
DSA_T = 256
DSA_HEAD_GROUP = 4
S_TC = 16
S_PAIR = 2

COL_QC, COL_QLAT, COL_KC, COL_VC, COL_KROPE, COL_KI, COL_WI = 0, 1536, 2304, 2432, 2560, 2688, 2816
COL_U, COL_QI, COL_KVLAT, PROJ_WIDTH = 3072, 4096, 5120, 5632

NEG_BIG = -1e30
LOG2E = math.log2(math.e)
INT_MIN = -2 ** 31

_bf16 = jnp.bfloat16
_f32 = jnp.float32


def _tile(n, pref):
    t = min(pref, n)
    while n % t:
        t //= 2
    return t


def _params(*sem):
    return pltpu.CompilerParams(dimension_semantics=sem, vmem_limit_bytes=VMEM_LIMIT_BYTES)


def _mm_kernel(x_ref, w_ref, o_ref, *, act):
    y = jnp.dot(x_ref[...], w_ref[...], preferred_element_type=_f32)
    if act == "relu2":
        y = jnp.square(jnp.maximum(y, 0.0))
    o_ref[...] = y.astype(o_ref.dtype)


def matmul(x, w, out_dtype, act=None):
    m, k = x.shape
    n = w.shape[1]
    tm, tn = _tile(m, MM_TM), _tile(n, MM_TN)
    return pl.pallas_call(
        functools.partial(_mm_kernel, act=act),
        grid=(m // tm, n // tn),
        in_specs=[pl.BlockSpec((tm, k), lambda i, j: (i, 0)),
                  pl.BlockSpec((k, tn), lambda i, j: (0, j))],
        out_specs=pl.BlockSpec((tm, tn), lambda i, j: (i, j)),
        out_shape=jax.ShapeDtypeStruct((m, n), out_dtype),
        compiler_params=_params("parallel", "parallel"),
        name="matmul",
    )(x, w)


def _mm_acc_kernel(x_ref, w_ref, o_ref):
    y = jnp.dot(x_ref[...], w_ref[...], preferred_element_type=_f32)

    @pl.when(pl.program_id(2) == 0)
    def _():
        o_ref[...] = y

    @pl.when(pl.program_id(2) != 0)
    def _():
        o_ref[...] += y


def matmul_ksplit(x, w):
    m, k = x.shape
    n = w.shape[1]
    tm, tn, tk = min(MM_TM, m), min(MM_TN, n), min(MM_TK, k)
    return pl.pallas_call(
        _mm_acc_kernel,
        grid=(m // tm, n // tn, k // tk),
        in_specs=[pl.BlockSpec((tm, tk), lambda i, j, l: (i, l)),
                  pl.BlockSpec((tk, tn), lambda i, j, l: (l, j))],
        out_specs=pl.BlockSpec((tm, tn), lambda i, j, l: (i, j)),
        out_shape=jax.ShapeDtypeStruct((m, n), _f32),
        compiler_params=_params("parallel", "parallel", "arbitrary"),
        name="matmul_ksplit",
    )(x, w)


def _mm3_kernel(a_ref, b_ref, c_ref, wa_ref, wb_ref, wc_ref, o_ref):
    y = jnp.dot(a_ref[...], wa_ref[...], preferred_element_type=_f32)
    y += jnp.dot(b_ref[...], wb_ref[...], preferred_element_type=_f32)
    y += jnp.dot(c_ref[...], wc_ref[...], preferred_element_type=_f32)
    o_ref[...] = y


def matmul_concat3(a, b, c, wa, wb, wc):
    m = a.shape[0]
    n = wa.shape[1]
    tm, tn = _tile(m, MM_TM), _tile(n, MM_TN)
    row = lambda i, j: (i, 0)
    col = lambda i, j: (0, j)
    return pl.pallas_call(
        _mm3_kernel,
        grid=(m // tm, n // tn),
        in_specs=[pl.BlockSpec((tm, a.shape[1]), row), pl.BlockSpec((tm, b.shape[1]), row),
                  pl.BlockSpec((tm, c.shape[1]), row),
                  pl.BlockSpec((wa.shape[0], tn), col), pl.BlockSpec((wb.shape[0], tn), col),
                  pl.BlockSpec((wc.shape[0], tn), col)],
        out_specs=pl.BlockSpec((tm, tn), lambda i, j: (i, j)),
        out_shape=jax.ShapeDtypeStruct((m, n), _f32),
        compiler_params=_params("parallel", "parallel"),
        name="matmul_concat3",
    )(a, b, c, wa, wb, wc)


def _ln_rows(z, g, b):
    mu = jnp.mean(z, axis=-1, keepdims=True)
    zc = z - mu
    var = jnp.mean(zc * zc, axis=-1, keepdims=True)
    return zc * lax.rsqrt(var + LN_EPS) * g + b


def _ln_kernel(x_ref, g_ref, b_ref, o_ref, ob_ref):
    y = _ln_rows(x_ref[...], g_ref[...], b_ref[...])
    o_ref[...] = y
    ob_ref[...] = y.astype(_bf16)


def _res_ln_kernel(h_ref, y_ref, g_ref, b_ref, o_ref, ob_ref, *, alpha):
    y = _ln_rows(alpha * h_ref[...] + y_ref[...], g_ref[...], b_ref[...])
    o_ref[...] = y
    ob_ref[...] = y.astype(_bf16)


def layer_norm(x, g, b, resid=None, alpha=1.0):
    m, d = x.shape
    tr = min(ROW_TILE, m)
    rows = pl.BlockSpec((tr, d), lambda i: (i, 0))
    vec = pl.BlockSpec((1, d), lambda i: (0, 0))
    out_shape = (jax.ShapeDtypeStruct((m, d), _f32), jax.ShapeDtypeStruct((m, d), _bf16))
    g2, b2 = g.reshape(1, d), b.reshape(1, d)
    if resid is None:
        return pl.pallas_call(_ln_kernel, grid=(m // tr,), in_specs=[rows, vec, vec],
                              out_specs=(rows, rows), out_shape=out_shape,
                              compiler_params=_params("parallel"), name="layer_norm")(x, g2, b2)
    return pl.pallas_call(functools.partial(_res_ln_kernel, alpha=alpha), grid=(m // tr,),
                          in_specs=[rows, rows, vec, vec], out_specs=(rows, rows),
                          out_shape=out_shape, compiler_params=_params("parallel"),
                          name="residual_layer_norm")(resid, x, g2, b2)


def _rope_cos_sin(length, dim):
    inv = ROPE_THETA ** (-jnp.arange(0, dim, 2, dtype=_f32) / dim)
    ang = jnp.arange(length, dtype=_f32)[:, None] * inv[None, :]
    return jnp.cos(ang), jnp.sin(ang)


def _rope_tables(length, dim, lead, width):
    cos, sin = _rope_cos_sin(length, dim)
    half = dim // 2
    zeros = lambda n: jnp.zeros((length, n), _f32)
    tail = width - lead - dim
    c = jnp.concatenate([jnp.ones((length, lead), _f32), cos, cos, zeros(tail)], axis=1)
    s_up = jnp.concatenate([zeros(lead + half), sin, zeros(tail)], axis=1)
    s_dn = jnp.concatenate([zeros(lead), -sin, zeros(half + tail)], axis=1)
    return c, s_up, s_dn


def _apply_rope(x, c, s_up, s_dn, half):
    width = x.shape[-1]
    return x * c + pltpu.roll(x, half, 1) * s_up + pltpu.roll(x, width - half, 1) * s_dn


def _rms_rows(x, g):
    return x * lax.rsqrt(jnp.mean(x * x, axis=-1, keepdims=True) + RMS_EPS) * g


A_QW = 256


def _mla_q_kernel(x_ref, g_ref, w_ref, c_ref, su_ref, sd_ref, o_ref, xn_ref):
    @pl.when(pl.program_id(1) == 0)
    def _():
        xn_ref[...] = _rms_rows(x_ref[...], g_ref[...]).astype(_bf16)

    y = jnp.dot(xn_ref[...], w_ref[...], preferred_element_type=_f32)
    y = _apply_rope(y, c_ref[...], su_ref[...], sd_ref[...], A_ROPE // 2)
    o_ref[0, 0] = y.astype(_bf16)


def _mla_kv_kernel(x_ref, kr_ref, g_ref, w_ref, c_ref, su_ref, sd_ref, k_ref, v_ref, xn_ref):
    @pl.when(pl.program_id(1) == 0)
    def _():
        xn_ref[...] = _rms_rows(x_ref[...], g_ref[...]).astype(_bf16)

    y = jnp.dot(xn_ref[...], w_ref[...], preferred_element_type=_f32)
    kr = _apply_rope(kr_ref[...], c_ref[...], su_ref[...], sd_ref[...], A_ROPE // 2)
    k_ref[0, 0] = jnp.concatenate([y[:, :A_NOPE], kr], axis=1).astype(_bf16)
    v_ref[0, 0] = y[:, A_NOPE:].astype(_bf16)


def _fold_lanes(x, op):
    r = x[:, 0:LANES]
    for n in range(1, x.shape[1] // LANES):
        r = op(r, x[:, n * LANES:(n + 1) * LANES])
    return r


def _lane_tile(x, width):
    return jnp.concatenate([x] * (width // LANES), axis=1)


def _mla_flash_kernel(q_ref, k_ref, v_ref, o_ref, s_ref, mx_ref, l_ref, acc_ref, *, log2_scale, tile):
    i = pl.program_id(2)
    q = q_ref[0, 0]

    def scores(j):
        k = k_ref[0, 0, pl.ds(pl.multiple_of(j * tile, tile), tile), :]
        return lax.dot_general(q, k, (((1,), (1,)), ((), ())), preferred_element_type=_f32) * log2_scale

    def keep(j, s):
        s_ref[j] = s
        mx_ref[...] = jnp.maximum(mx_ref[...], _fold_lanes(s, jnp.maximum))

    mx_ref[...] = jnp.full(mx_ref.shape, NEG_BIG, _f32)

    def sweep1(j, carry):
        keep(j, scores(j))
        return carry

    lax.fori_loop(0, i, sweep1, 0)
    row = lax.broadcasted_iota(jnp.int32, (tile, tile), 0) // CHUNK
    col = lax.broadcasted_iota(jnp.int32, (tile, tile), 1) // CHUNK
    keep(i, jnp.where(col <= row, scores(i), NEG_BIG))

    mx_ref[...] = jnp.broadcast_to(jnp.max(mx_ref[...], axis=1, keepdims=True), mx_ref.shape)
    l_ref[...] = jnp.zeros(l_ref.shape, _f32)
    acc_ref[...] = jnp.zeros(acc_ref.shape, _f32)

    def sweep2(j, carry):
        p = jnp.exp2(s_ref[j] - _lane_tile(mx_ref[...], tile))
        l_ref[...] += _fold_lanes(p, jnp.add)
        v = v_ref[0, 0, pl.ds(pl.multiple_of(j * tile, tile), tile), :]
        acc_ref[...] += jnp.dot(p.astype(_bf16), v, preferred_element_type=_f32)
        return carry

    lax.fori_loop(0, i + 1, sweep2, 0)
    o_ref[...] = (acc_ref[...] / jnp.sum(l_ref[...], axis=1, keepdims=True)).astype(o_ref.dtype)


def mla_mixer(proj, bsz, length, g_q, w_uq, g_kv, w_ukv):
    m = bsz * length
    tm = min(PROJ_TM, length)
    nt = length // tm
    heads = A_HEADS
    qd = A_NOPE + A_ROPE
    wq = jnp.pad(w_uq.reshape(A_Q_LORA, heads, qd), ((0, 0), (0, 0), (0, A_QW - qd)))
    wq = wq.reshape(A_Q_LORA, heads * A_QW).astype(_bf16)
    wkv = w_ukv.astype(_bf16)
    qc, qsu, qsd = _rope_tables(length, A_ROPE, A_NOPE, A_QW)
    kc, ksu, ksd = _rope_tables(length, A_ROPE, 0, LANES)

    tab_q = pl.BlockSpec((tm, A_QW), lambda i, h: (i % nt, 0))
    tab_k = pl.BlockSpec((tm, LANES), lambda i, h: (i % nt, 0))
    head_out = lambda w: pl.BlockSpec((1, 1, tm, w), lambda i, h: (i // nt, h, i % nt, 0))

    q = pl.pallas_call(
        _mla_q_kernel,
        grid=(m // tm, heads),
        in_specs=[pl.BlockSpec((tm, A_Q_LORA), lambda i, h: (i, COL_QLAT // A_Q_LORA)),
                  pl.BlockSpec((1, A_Q_LORA), lambda i, h: (0, 0)),
                  pl.BlockSpec((A_Q_LORA, A_QW), lambda i, h: (0, h)),
                  tab_q, tab_q, tab_q],
        out_specs=head_out(A_QW),
        out_shape=jax.ShapeDtypeStruct((bsz, heads, length, A_QW), _bf16),
        scratch_shapes=[pltpu.VMEM((tm, A_Q_LORA), _bf16)],
        compiler_params=_params("parallel", "arbitrary"),
        name="mla_q_proj",
    )(proj, g_q.reshape(1, -1), wq, qc, qsu, qsd)

    k, v = pl.pallas_call(
        _mla_kv_kernel,
        grid=(m // tm, heads),
        in_specs=[pl.BlockSpec((tm, A_KV_LORA), lambda i, h: (i, COL_KVLAT // A_KV_LORA)),
                  pl.BlockSpec((tm, LANES), lambda i, h: (i, COL_KROPE // LANES)),
                  pl.BlockSpec((1, A_KV_LORA), lambda i, h: (0, 0)),
                  pl.BlockSpec((A_KV_LORA, A_NOPE + A_VDIM), lambda i, h: (0, h)),
                  tab_k, tab_k, tab_k],
        out_specs=(head_out(A_QW), head_out(A_VDIM)),
        out_shape=(jax.ShapeDtypeStruct((bsz, heads, length, A_QW), _bf16),
                   jax.ShapeDtypeStruct((bsz, heads, length, A_VDIM), _bf16)),
        scratch_shapes=[pltpu.VMEM((tm, A_KV_LORA), _bf16)],
        compiler_params=_params("parallel", "arbitrary"),
        name="mla_kv_proj",
    )(proj, proj, g_kv.reshape(1, -1), wkv, kc, ksu, ksd)

    t = min(ATT_T, length)
    nq = length // t
    return pl.pallas_call(
        functools.partial(_mla_flash_kernel, log2_scale=float(qd) ** -0.5 * LOG2E, tile=t),
        grid=(bsz, heads, nq),
        in_specs=[pl.BlockSpec((1, 1, t, A_QW), lambda b, h, i: (b, h, i, 0)),
                  pl.BlockSpec((1, 1, length, A_QW), lambda b, h, i: (b, h, 0, 0)),
                  pl.BlockSpec((1, 1, length, A_VDIM), lambda b, h, i: (b, h, 0, 0))],
        out_specs=pl.BlockSpec((t, A_VDIM), lambda b, h, i: (b * nq + i, h)),
        out_shape=jax.ShapeDtypeStruct((m, heads * A_VDIM), _bf16),
        scratch_shapes=[pltpu.VMEM((nq, t, t), _f32), pltpu.VMEM((t, LANES), _f32),
                        pltpu.VMEM((t, LANES), _f32), pltpu.VMEM((t, A_VDIM), _f32)],
        compiler_params=_params("parallel", "parallel", "arbitrary"),
        name="mla_flash",
    )(q, k, v)


def _s5_operators(lam_re, lam_im, log_step, b_re, b_im, c_re, c_im, d_skip):
    hp = lax.Precision.HIGHEST
    g, n, p, tc = S_GROUPS, S_STATE, S_GROUP_CH, S_TC
    lr, li = lam_re.astype(_f32), lam_im.astype(_f32)
    dt = jnp.exp(log_step.astype(_f32))[:, None]
    steps = jnp.arange(tc + 1, dtype=_f32)[:, None, None]
    mag = jnp.exp(lr * dt * steps)
    pw_re, pw_im = mag * jnp.cos(li * dt * steps), mag * jnp.sin(li * dt * steps)
    ab_re, ab_im = pw_re[1], pw_im[1]
    den = lr * lr + li * li
    nr, ni = ab_re - 1.0, ab_im
    co_re = (nr * lr + ni * li) / den
    co_im = (ni * lr - nr * li) / den
    br, bi = b_re.astype(_f32), b_im.astype(_f32)
    bb_re = co_re[..., None] * br - co_im[..., None] * bi
    bb_im = co_re[..., None] * bi + co_im[..., None] * br
    cr, ci = c_re.astype(_f32), c_im.astype(_f32)
    cp_re = cr[None] * pw_re[:, :, None, :] - ci[None] * pw_im[:, :, None, :]
    cp_im = cr[None] * pw_im[:, :, None, :] + ci[None] * pw_re[:, :, None, :]
    kern = (jnp.einsum("dgpn,gnq->dgpq", cp_re, bb_re, precision=hp)
            - jnp.einsum("dgpn,gnq->dgpq", cp_im, bb_im, precision=hp))
    s_idx = jnp.arange(tc)[:, None]
    t_idx = jnp.arange(tc)[None, :]
    lag = t_idx - s_idx
    toe = kern[jnp.clip(lag, 0, tc)]
    toe = jnp.where((lag >= 0)[:, :, None, None, None], toe, 0.0)
    eye = (lag == 0)[:, :, None, None, None] * jnp.eye(p, dtype=_f32)[None, None, None]
    toe = toe + eye * d_skip.astype(_f32)[None, None, :, :, None]
    mg = jnp.transpose(toe, (2, 0, 4, 1, 3)).reshape(g, tc * p, tc * p)
    pr, pi_ = pw_re[tc - 1 - jnp.arange(tc)], pw_im[tc - 1 - jnp.arange(tc)]
    bs_re = pr[..., None] * bb_re[None] - pi_[..., None] * bb_im[None]
    bs_im = pr[..., None] * bb_im[None] + pi_[..., None] * bb_re[None]
    bs_re = jnp.transpose(bs_re, (1, 0, 3, 2)).reshape(g, tc * p, n)
    bs_im = jnp.transpose(bs_im, (1, 0, 3, 2)).reshape(g, tc * p, n)
    cs_re = jnp.transpose(cp_re[1:], (1, 3, 0, 2)).reshape(g, n, tc * p)
    cs_im = -jnp.transpose(cp_im[1:], (1, 3, 0, 2)).reshape(g, n, tc * p)

    def blockdiag(a):
        a = a.reshape(g // S_PAIR, S_PAIR, a.shape[1], a.shape[2])
        z = jnp.zeros_like(a[:, 0])
        top = jnp.concatenate([a[:, 0], z], axis=2)
        bot = jnp.concatenate([z, a[:, 1]], axis=2)
        return jnp.concatenate([top, bot], axis=1)

    mp = blockdiag(mg).astype(_bf16)
    bp = jnp.concatenate([blockdiag(bs_re), blockdiag(bs_im)], axis=2).astype(_bf16)
    cp = jnp.concatenate([blockdiag(cs_re), blockdiag(cs_im)], axis=1).astype(_bf16)
    ap = jnp.stack([pw_re[tc].reshape(g // S_PAIR, S_PAIR * n),
                    pw_im[tc].reshape(g // S_PAIR, S_PAIR * n)], axis=1)
    return mp, bp, cp, ap


def _gelu_tanh(x):
    return 0.5 * x * (1.0 + jnp.tanh(math.sqrt(2.0 / math.pi) * (x + 0.044715 * (x * x * x))))


def _s5_kernel(u_ref, mp_ref, bp_ref, cp_ref, ap_ref, o_ref, s_ref, hp_ref, *, nchunks):
    u = u_ref[0, 0]
    s_ref[...] = jnp.dot(u, bp_ref[0], preferred_element_type=_f32)
    a_re = ap_ref[0, 0:1, :]
    a_im = ap_ref[0, 1:2, :]

    def step(c8, carry):
        h_re, h_im = carry
        base = pl.multiple_of(c8 * SUBLANES, SUBLANES)
        s_blk = s_ref[pl.ds(base, SUBLANES), :]
        rows_re, rows_im = [], []
        for r in range(SUBLANES):
            rows_re.append(h_re)
            rows_im.append(h_im)
            s_re = s_blk[r:r + 1, 0:LANES]
            s_im = s_blk[r:r + 1, LANES:2 * LANES]
            h_re, h_im = a_re * h_re - a_im * h_im + s_re, a_re * h_im + a_im * h_re + s_im
        hp_ref[pl.ds(base, SUBLANES), :] = jnp.concatenate(
            [jnp.concatenate(rows_re, axis=0), jnp.concatenate(rows_im, axis=0)], axis=1)
        return h_re, h_im

    zero = jnp.zeros((1, LANES), _f32)
    lax.fori_loop(0, nchunks // SUBLANES, step, (zero, zero))
    y = jnp.dot(u, mp_ref[0], preferred_element_type=_f32)
    y += jnp.dot(hp_ref[...].astype(_bf16), cp_ref[0], preferred_element_type=_f32)
    o_ref[0, 0] = _gelu_tanh(y)


def _glu_kernel(g_ref, w_ref, b_ref, o_ref):
    g = g_ref[...]
    z = jnp.dot(g.astype(_bf16), w_ref[...], preferred_element_type=_f32) + b_ref[...]
    o_ref[...] = (g * jax.nn.sigmoid(z)).astype(o_ref.dtype)


def s5_mixer(proj, bsz, length, lam_re, lam_im, log_step, b_re, b_im, c_re, c_im, d_skip, w_glu, b_glu):
    m = bsz * length
    tc, p = S_TC, S_GROUP_CH
    npair = S_GROUPS // S_PAIR
    nch = length // tc
    wide = S_PAIR * tc * p
    mp, bp, cp, ap = _s5_operators(lam_re, lam_im, log_step, b_re, b_im, c_re, c_im, d_skip)
    u = proj[:, COL_U:COL_U + S_WIDTH].reshape(bsz, nch, tc, npair, S_PAIR, p)
    u = jnp.transpose(u, (0, 3, 1, 4, 2, 5)).reshape(bsz, npair, nch, wide).astype(_bf16)
    g = pl.pallas_call(
        functools.partial(_s5_kernel, nchunks=nch),
        grid=(npair, bsz),
        in_specs=[pl.BlockSpec((1, 1, nch, wide), lambda gp, b: (b, gp, 0, 0)),
                  pl.BlockSpec((1, wide, wide), lambda gp, b: (gp, 0, 0)),
                  pl.BlockSpec((1, wide, 2 * LANES), lambda gp, b: (gp, 0, 0)),
                  pl.BlockSpec((1, 2 * LANES, wide), lambda gp, b: (gp, 0, 0)),
                  pl.BlockSpec((1, 2, LANES), lambda gp, b: (gp, 0, 0))],
        out_specs=pl.BlockSpec((1, 1, nch, wide), lambda gp, b: (b, gp, 0, 0)),
        out_shape=jax.ShapeDtypeStruct((bsz, npair, nch, wide), _f32),
        scratch_shapes=[pltpu.VMEM((nch, 2 * LANES), _f32), pltpu.VMEM((nch, 2 * LANES), _f32)],
        compiler_params=_params("parallel", "parallel"),
        name="s5_scan",
    )(u, mp, bp, cp, ap)
    g = g.reshape(bsz, npair, nch, S_PAIR, tc, p)
    g = jnp.transpose(g, (0, 2, 4, 1, 3, 5)).reshape(m, S_WIDTH)
    tm = min(MM_TM, m)
    return pl.pallas_call(
        _glu_kernel,
        grid=(m // tm,),
        in_specs=[pl.BlockSpec((tm, S_WIDTH), lambda i: (i, 0)),
                  pl.BlockSpec((S_WIDTH, S_WIDTH), lambda i: (0, 0)),
                  pl.BlockSpec((1, S_WIDTH), lambda i: (0, 0))],
        out_specs=pl.BlockSpec((tm, S_WIDTH), lambda i: (i, 0)),
        out_shape=jax.ShapeDtypeStruct((m, S_WIDTH), _bf16),
        compiler_params=_params("parallel"),
        name="s5_glu",
    )(g, w_glu.astype(_bf16), b_glu.reshape(1, -1).astype(_f32))


def _t5_bucket(rel):
    half = T5_BUCKETS // 2
    max_exact = half // 2
    ret = jnp.where(rel > 0, half, 0)
    n = jnp.abs(rel)
    nf = jnp.maximum(n, 1).astype(_f32)
    large = max_exact + (jnp.log(nf / max_exact) / math.log(T5_MAX_DIST / max_exact)
                         * (half - max_exact)).astype(jnp.int32)
    large = jnp.minimum(large, half - 1)
    return ret + jnp.where(n < max_exact, n, large)


def _dsa_bias_tiles(rel_bias, t):
    assert t >= T5_MAX_DIST
    hp = lax.Precision.HIGHEST
    r = jnp.arange(t, dtype=jnp.int32)[:, None]
    c = jnp.arange(t, dtype=jnp.int32)[None, :]
    rel = jnp.stack([c - r - t, c - r])
    onehot = (_t5_bucket(rel)[..., None] == jnp.arange(T5_BUCKETS)).astype(_f32)
    far = rel_bias.astype(_f32)[T5_BUCKETS // 2 - 1]
    tiles = jnp.einsum("xrcb,bh->hxrc", onehot, rel_bias.astype(_f32) - far[None, :], precision=hp)
    return tiles * LOG2E


def _dsa_prep_kernel(qi_ref, ki_ref, qc_ref, qsu_ref, qsd_ref, kc_ref, ksu_ref, ksd_ref, qo_ref, ko_ref):
    half = I_ROPE // 2
    rep = (I_HEADS * I_DIM) // LANES
    tile = lambda r: jnp.concatenate([r[...]] * rep, axis=1)
    q = _apply_rope(qi_ref[...], tile(qc_ref), tile(qsu_ref), tile(qsd_ref), half)
    for h in range(I_HEADS):
        qo_ref[0, h] = q[:, h * I_DIM:(h + 1) * I_DIM].astype(_bf16)
    ko_ref[...] = _apply_rope(ki_ref[...], kc_ref[...], ksu_ref[...], ksd_ref[...], half).astype(_bf16)


def _float_key(x):
    b = lax.bitcast_convert_type(x, jnp.int32)
    return b ^ ((b >> 31) & 0x7FFFFFFF)


def _dsa_kernel(qi_ref, wi_ref, kit_ref, qc_ref, kct_ref, vc_ref, bias_ref, o_ref,
                keys_ref, s_ref, wb_ref, qb_ref, mx_ref, l_ref, acc_ref, *, tile, k_sel, log2_scale, idx_scale):
    i = pl.program_id(1)

    w = wi_ref[...]
    for h in range(I_HEADS):
        wb_ref[h] = jnp.broadcast_to(w[:, h:h + 1], (tile, LANES))
    qf = qc_ref[...]
    for h in range(C_HEADS):
        qb_ref[h] = qf[:, h * C_HEAD_DIM:(h + 1) * C_HEAD_DIM].astype(_bf16)

    row_chunk = lax.broadcasted_iota(jnp.int32, (tile, tile), 0) // CHUNK
    col_chunk = lax.broadcasted_iota(jnp.int32, (tile, tile), 1) // CHUNK
    admissible_diag = col_chunk <= row_chunk

    def score_block(j, carry):
        kt = kit_ref[0, j]
        acc = jnp.zeros((tile, tile), _f32)
        for h in range(I_HEADS):
            d = jnp.dot(qi_ref[0, h], kt, preferred_element_type=_f32)
            acc += jnp.maximum(d, 0.0) * _lane_tile(wb_ref[h], tile)
        key = _float_key(acc * idx_scale)
        keys_ref[j] = jnp.where(jnp.logical_or(j < i, admissible_diag), key, INT_MIN)
        return carry

    lax.fori_loop(0, i + 1, score_block, 0)

    def count_ge(t):
        tb = jnp.broadcast_to(t, (tile, LANES))

        def blk(j, c):
            kj = keys_ref[j]
            for s in range(tile // LANES):
                c += jnp.where(kj[:, s * LANES:(s + 1) * LANES] >= tb, 1, 0)
            return c

        c = lax.fori_loop(0, i + 1, blk, jnp.zeros((tile, LANES), jnp.int32))
        return jnp.sum(c, axis=1, keepdims=True)

    def search_cond(carry):
        bit, _, _, settled = carry
        return jnp.logical_and(bit < 32, settled == 0)

    def search_step(carry):
        bit, t, cnt_t, _ = carry
        cand = t + jnp.left_shift(jnp.int32(1), 31 - bit)
        cnt = count_ge(cand)
        take = cnt >= k_sel
        t = jnp.where(take, cand, t)
        cnt_t = jnp.where(take, cnt, cnt_t)
        settled = jnp.min(jnp.where(cnt_t == k_sel, 1, 0))
        return bit + 1, t, cnt_t, settled

    all_selected = ((i + 1) * tile <= k_sel).astype(jnp.int32)
    _, thr, _, _ = lax.while_loop(
        search_cond, search_step,
        (jnp.int32(0), jnp.full((tile, 1), INT_MIN, jnp.int32), jnp.full((tile, 1), -1, jnp.int32), all_selected))
    thr = jnp.maximum(thr, INT_MIN + 1)
    thr_b = jnp.broadcast_to(thr, (tile, tile))

    def mask_block(j, carry):
        mb = jnp.where(keys_ref[j] >= thr_b, 0.0, NEG_BIG).astype(_f32)
        keys_ref[j] = lax.bitcast_convert_type(mb, jnp.int32)
        return carry

    lax.fori_loop(0, i + 1, mask_block, 0)

    grp = DSA_HEAD_GROUP
    rows = grp * tile

    def group_body(g, carry):
        h0 = pl.multiple_of(g * grp, grp)
        q = qb_ref[pl.ds(h0, grp)].reshape(rows, C_HEAD_DIM)
        mx_ref[...] = jnp.full(mx_ref.shape, NEG_BIG, _f32)

        def scores(j):
            s = jnp.dot(q, kct_ref[0, j], preferred_element_type=_f32) * log2_scale
            return s.reshape(grp, tile, tile) + lax.bitcast_convert_type(keys_ref[j], _f32)[None]

        def keep(j, s):
            s_ref[:, j] = s
            mx_ref[...] = jnp.maximum(mx_ref[...], _fold_lanes(s.reshape(rows, tile), jnp.maximum))

        def far(j, cy):
            keep(j, scores(j))
            return cy

        lax.fori_loop(0, jnp.maximum(i - 1, 0), far, 0)

        @pl.when(i >= 1)
        def _():
            keep(i - 1, scores(i - 1) + bias_ref[pl.ds(h0, grp), 0])

        keep(i, scores(i) + bias_ref[pl.ds(h0, grp), 1])

        mx_ref[...] = jnp.broadcast_to(jnp.max(mx_ref[...], axis=1, keepdims=True), mx_ref.shape)
        l_ref[...] = jnp.zeros(l_ref.shape, _f32)
        acc_ref[pl.ds(h0, grp)] = jnp.zeros((grp, tile, C_HEAD_DIM), _f32)

        def sweep2(j, cy):
            p = jnp.exp2(s_ref[:, j].reshape(rows, tile) - _lane_tile(mx_ref[...], tile))
            l_ref[...] += _fold_lanes(p, jnp.add)
            pv = jnp.dot(p.astype(_bf16), vc_ref[0, j], preferred_element_type=_f32)
            acc_ref[pl.ds(h0, grp)] += pv.reshape(grp, tile, C_HEAD_DIM)
            return cy

        lax.fori_loop(0, i + 1, sweep2, 0)
        denom = jnp.sum(l_ref[...], axis=1, keepdims=True).reshape(grp, tile, 1)
        acc_ref[pl.ds(h0, grp)] = acc_ref[pl.ds(h0, grp)] / denom
        return carry

    lax.fori_loop(0, C_HEADS // grp, group_body, 0)
    for h in range(C_HEADS):
        o_ref[:, h * C_HEAD_DIM:(h + 1) * C_HEAD_DIM] = acc_ref[h].astype(o_ref.dtype)


def dsa_mixer(proj, bsz, length, rel_bias):
    m = bsz * length
    t = min(DSA_T, length)
    nb = length // t
    k_sel = min(TOPK_MAX, length // 4)
    tm = min(PROJ_TM, length)
    nt = length // tm
    iw = I_HEADS * I_DIM
    qtab = _rope_tables(length, I_ROPE, 0, I_DIM)
    qtab = [jnp.concatenate([a.at[:, I_ROPE:].set(1.0) if n == 0 else a] * (LANES // I_DIM), axis=1)
            for n, a in enumerate(qtab)]
    ktab = _rope_tables(length, I_ROPE, 0, LANES)
    ktab = [ktab[0].at[:, I_ROPE:I_DIM].set(1.0), ktab[1], ktab[2]]
    tab = pl.BlockSpec((tm, LANES), lambda i: (i % nt, 0))
    qi, ki = pl.pallas_call(
        _dsa_prep_kernel,
        grid=(m // tm,),
        in_specs=[pl.BlockSpec((tm, iw), lambda i: (i, COL_QI // iw)),
                  pl.BlockSpec((tm, LANES), lambda i: (i, COL_KI // LANES)),
                  tab, tab, tab, tab, tab, tab],
        out_specs=(pl.BlockSpec((1, I_HEADS, tm, I_DIM), lambda i: (i // nt, 0, i % nt, 0)),
                   pl.BlockSpec((tm, LANES), lambda i: (i, 0))),
        out_shape=(jax.ShapeDtypeStruct((bsz, I_HEADS, length, I_DIM), _bf16),
                   jax.ShapeDtypeStruct((m, LANES), _bf16)),
        compiler_params=_params("parallel"),
        name="dsa_prep",
    )(proj, proj, *qtab, *ktab)

    kit = jnp.transpose(ki[:, :I_DIM].reshape(bsz, nb, t, I_DIM), (0, 1, 3, 2))
    kct = jnp.transpose(proj[:, COL_KC:COL_KC + C_HEAD_DIM].astype(_bf16).reshape(bsz, nb, t, C_HEAD_DIM),
                        (0, 1, 3, 2))
    vc = proj[:, COL_VC:COL_VC + C_HEAD_DIM].astype(_bf16).reshape(bsz, nb, t, C_HEAD_DIM)
    bias = _dsa_bias_tiles(rel_bias, t)
    qw = C_HEADS * C_HEAD_DIM
    whole = lambda shape: pl.BlockSpec((1,) + shape, lambda b, i: (b, 0, 0, 0))
    return pl.pallas_call(
        functools.partial(_dsa_kernel, tile=t, k_sel=k_sel, log2_scale=float(C_HEAD_DIM) ** -0.5 * LOG2E,
                          idx_scale=float(I_DIM * I_HEADS) ** -0.5),
        grid=(bsz, nb),
        in_specs=[pl.BlockSpec((1, I_HEADS, t, I_DIM), lambda b, i: (b, 0, i, 0)),
                  pl.BlockSpec((t, LANES), lambda b, i: (b * nb + i, COL_WI // LANES)),
                  whole((nb, I_DIM, t)),
                  pl.BlockSpec((t, qw), lambda b, i: (b * nb + i, COL_QC // qw)),
                  whole((nb, C_HEAD_DIM, t)),
                  whole((nb, t, C_HEAD_DIM)),
                  pl.BlockSpec((C_HEADS, 2, t, t), lambda b, i: (0, 0, 0, 0),
                               pipeline_mode=pl.Buffered(1))],
        out_specs=pl.BlockSpec((t, qw), lambda b, i: (b * nb + i, 0)),
        out_shape=jax.ShapeDtypeStruct((m, qw), _bf16),
        scratch_shapes=[pltpu.VMEM((nb, t, t), jnp.int32),
                        pltpu.VMEM((DSA_HEAD_GROUP, nb, t, t), _f32),
                        pltpu.VMEM((I_HEADS, t, LANES), _f32),
                        pltpu.VMEM((C_HEADS, t, C_HEAD_DIM), _bf16),
                        pltpu.VMEM((DSA_HEAD_GROUP * t, LANES), _f32),
                        pltpu.VMEM((DSA_HEAD_GROUP * t, LANES), _f32),
                        pltpu.VMEM((C_HEADS, t, C_HEAD_DIM), _f32)],
        compiler_params=_params("parallel", "arbitrary"),
        name="dsa_attention",
    )(qi, proj, kit, proj, kct, vc, bias)


def _reorder_w_in(w):
    sizes = (A_Q_LORA, A_KV_LORA, A_ROPE, S_WIDTH, C_HEADS * C_HEAD_DIM, C_HEAD_DIM, C_HEAD_DIM,
             I_HEADS * I_DIM, I_DIM, I_HEADS)
    offs = [0]
    for s in sizes:
        offs.append(offs[-1] + s)
    q_lat, kv_lat, k_rope, u, q_c, k_c, v_c, q_i, k_i, w_i = [w[:, offs[n]:offs[n + 1]] for n in range(10)]
    pad = lambda a, width: jnp.pad(a, ((0, 0), (0, width - a.shape[1])))
    cols = [q_c, q_lat, k_c, v_c, pad(k_rope, LANES), pad(k_i, LANES), pad(w_i, LANES),
            jnp.zeros((w.shape[0], LANES), w.dtype), u, q_i, kv_lat]
    out = jnp.concatenate(cols, axis=1).astype(_bf16)
    assert out.shape[1] == PROJ_WIDTH
    return out


def kernel(x, ln_in_g, ln_in_b, rel_bias, w_in, a_gq, a_wuq, a_gkv, a_wukv, s_lam_re, s_lam_im, s_log_step, s_b_re, s_b_im, s_c_re, s_c_im, s_d, s_w_glu, s_b_glu, w_out, ln1_g, ln1_b, w_ff1, w_ff2, ln2_g, ln2_b):
    bsz, length, d_model = x.shape
    depth = w_in.shape[0]
    alpha = (2 * depth) ** 0.25
    m = bsz * length
    a_w, b_w = A_HEADS * A_VDIM, S_WIDTH

    h, hb = layer_norm(x.reshape(m, d_model), ln_in_g, ln_in_b)
    for l in range(depth):
        proj = matmul(hb, _reorder_w_in(w_in[l]), _f32)
        a_out = mla_mixer(proj, bsz, length, a_gq[l], a_wuq[l], a_gkv[l], a_wukv[l])
        b_out = s5_mixer(proj, bsz, length, s_lam_re[l], s_lam_im[l], s_log_step[l], s_b_re[l], s_b_im[l],
                         s_c_re[l], s_c_im[l], s_d[l], s_w_glu[l], s_b_glu[l])
        c_out = dsa_mixer(proj, bsz, length, rel_bias)
        wo = w_out[l].astype(_bf16)
        mix = matmul_concat3(a_out, b_out, c_out, wo[:a_w], wo[a_w:a_w + b_w], wo[a_w + b_w:])
        h, hb = layer_norm(mix, ln1_g[l], ln1_b[l], resid=h, alpha=alpha)
        act = matmul(hb, w_ff1[l].astype(_bf16), _bf16, act="relu2")
        ff = matmul_ksplit(act, w_ff2[l].astype(_bf16))
        h, hb = layer_norm(ff, ln2_g[l], ln2_b[l], resid=h, alpha=alpha)
    return h.reshape(bsz, length, d_model)
```

```python
import functools
import math

import jax
import jax.numpy as jnp
from jax import lax
from jax.experimental import pallas as pl
from jax.experimental.pallas import tpu as pltpu

CHUNK = 64
A_HEADS, A_NOPE, A_ROPE, A_VDIM = 12, 128, 64, 128
A_Q_LORA, A_KV_LORA = 768, 512
S_GROUP_CH, S_WIDTH, S_STATE = 16, 1024, 64
S_GROUPS = S_WIDTH // S_GROUP_CH
C_HEADS, C_HEAD_DIM = 12, 128
I_HEADS, I_DIM, I_ROPE = 16, 64, 32
TOPK_MAX = 256
T5_BUCKETS, T5_MAX_DIST = 32, 128
ROPE_THETA = 10000.0
LN_EPS = 1e-5
RMS_EPS = 1e-6

LANES = 128
SUBLANES = 8
VMEM_LIMIT_BYTES = 56 * 1024 * 1024

MM_TM, MM_TN, MM_TK = 1024, 1024, 4096
ROW_TILE = 256
PROJ_TM = 512
ATT_T = 512
DSA_T = 256
DSA_HEAD_GROUP = 4
S_TC = 16
S_LB_GROUPS = LANES // S_GROUP_CH

COL_QC, COL_QLAT, COL_KC, COL_VC, COL_KROPE, COL_KI, COL_WI = 0, 1536, 2304, 2432, 2560, 2688, 2816
COL_U, COL_QI, COL_KVLAT, PROJ_WIDTH = 3072, 4096, 5120, 5632

NEG_BIG = -1e30
LOG2E = math.log2(math.e)
INT_MIN = -2 ** 31

_bf16 = jnp.bfloat16
_f32 = jnp.float32


def _tile(n, pref):
    t = min(pref, n)
    while n % t:
        t //= 2
    return t


def _params(*sem):
    return pltpu.CompilerParams(dimension_semantics=sem, vmem_limit_bytes=VMEM_LIMIT_BYTES)


def _mm_kernel(x_ref, w_ref, o_ref, *, act):
    y = jnp.dot(x_ref[...], w_ref[...], preferred_element_type=_f32)
    if act == "relu2":
        y = jnp.square(jnp.maximum(y, 0.0))
    o_ref[...] = y.astype(o_ref.dtype)


def matmul(x, w, out_dtype, act=None):
    m, k = x.shape
    n = w.shape[1]
    tm, tn = _tile(m, MM_TM), _tile(n, MM_TN)
    return pl.pallas_call(
        functools.partial(_mm_kernel, act=act),
        grid=(m // tm, n // tn),
        in_specs=[pl.BlockSpec((tm, k), lambda i, j: (i, 0)),
                  pl.BlockSpec((k, tn), lambda i, j: (0, j))],
        out_specs=pl.BlockSpec((tm, tn), lambda i, j: (i, j)),
        out_shape=jax.ShapeDtypeStruct((m, n), out_dtype),
        compiler_params=_params("parallel", "parallel"),
        name="matmul",
    )(x, w)


def _mm_acc_kernel(x_ref, w_ref, o_ref):
    y = jnp.dot(x_ref[...], w_ref[...], preferred_element_type=_f32)

    @pl.when(pl.program_id(2) == 0)
    def _():
        o_ref[...] = y

    @pl.when(pl.program_id(2) != 0)
    def _():
        o_ref[...] += y


def matmul_ksplit(x, w):
    m, k = x.shape
    n = w.shape[1]
    tm, tn, tk = min(MM_TM, m), min(MM_TN, n), min(MM_TK, k)
    return pl.pallas_call(
        _mm_acc_kernel,
        grid=(m // tm, n // tn, k // tk),
        in_specs=[pl.BlockSpec((tm, tk), lambda i, j, l: (i, l)),
                  pl.BlockSpec((tk, tn), lambda i, j, l: (l, j))],
        out_specs=pl.BlockSpec((tm, tn), lambda i, j, l: (i, j)),
        out_shape=jax.ShapeDtypeStruct((m, n), _f32),
        compiler_params=_params("parallel", "parallel", "arbitrary"),
        name="matmul_ksplit",
    )(x, w)


def _mm3_kernel(a_ref, b_ref, c_ref, w_ref, o_ref):
    ka, kb = a_ref.shape[1], b_ref.shape[1]
    y = jnp.dot(a_ref[...], w_ref[0:ka, :], preferred_element_type=_f32)
    y += jnp.dot(b_ref[...], w_ref[ka:ka + kb, :], preferred_element_type=_f32)
    y += jnp.dot(c_ref[...], w_ref[ka + kb:, :], preferred_element_type=_f32)
    o_ref[...] = y


def matmul_concat3(a, b, c, w):
    m = a.shape[0]
    k, n = w.shape
    assert a.shape[1] + b.shape[1] + c.shape[1] == k
    tm, tn = _tile(m, MM_TM), _tile(n, MM_TN)
    row = lambda i, j: (i, 0)
    return pl.pallas_call(
        _mm3_kernel,
        grid=(m // tm, n // tn),
        in_specs=[pl.BlockSpec((tm, a.shape[1]), row), pl.BlockSpec((tm, b.shape[1]), row),
                  pl.BlockSpec((tm, c.shape[1]), row),
                  pl.BlockSpec((k, tn), lambda i, j: (0, j))],
        out_specs=pl.BlockSpec((tm, tn), lambda i, j: (i, j)),
        out_shape=jax.ShapeDtypeStruct((m, n), _f32),
        compiler_params=_params("parallel", "parallel"),
        name="matmul_concat3",
    )(a, b, c, w)


def _ln_rows(z, g, b):
    mu = jnp.mean(z, axis=-1, keepdims=True)
    zc = z - mu
    var = jnp.mean(zc * zc, axis=-1, keepdims=True)
    return zc * lax.rsqrt(var + LN_EPS) * g + b


def _ln_kernel(x_ref, g_ref, b_ref, o_ref, ob_ref):
    y = _ln_rows(x_ref[...], g_ref[...], b_ref[...])
    o_ref[...] = y
    ob_ref[...] = y.astype(_bf16)


def _res_ln_kernel(h_ref, y_ref, g_ref, b_ref, o_ref, ob_ref, *, alpha):
    y = _ln_rows(alpha * h_ref[...] + y_ref[...], g_ref[...], b_ref[...])
    o_ref[...] = y
    ob_ref[...] = y.astype(_bf16)


def layer_norm(x, g, b, resid=None, alpha=1.0):
    m, d = x.shape
    tr = min(ROW_TILE, m)
    rows = pl.BlockSpec((tr, d), lambda i: (i, 0))
    vec = pl.BlockSpec((1, d), lambda i: (0, 0))
    out_shape = (jax.ShapeDtypeStruct((m, d), _f32), jax.ShapeDtypeStruct((m, d), _bf16))
    g2, b2 = g.reshape(1, d), b.reshape(1, d)
    if resid is None:
        return pl.pallas_call(_ln_kernel, grid=(m // tr,), in_specs=[rows, vec, vec],
                              out_specs=(rows, rows), out_shape=out_shape,
                              compiler_params=_params("parallel"), name="layer_norm")(x, g2, b2)
    return pl.pallas_call(functools.partial(_res_ln_kernel, alpha=alpha), grid=(m // tr,),
                          in_specs=[rows, rows, vec, vec], out_specs=(rows, rows),
                          out_shape=out_shape, compiler_params=_params("parallel"),
                          name="residual_layer_norm")(resid, x, g2, b2)


def _rope_cos_sin(length, dim):
    inv = ROPE_THETA ** (-jnp.arange(0, dim, 2, dtype=_f32) / dim)
    ang = jnp.arange(length, dtype=_f32)[:, None] * inv[None, :]
    return jnp.cos(ang), jnp.sin(ang)


def _rope_tables(length, dim, lead, width):
    cos, sin = _rope_cos_sin(length, dim)
    half = dim // 2
    zeros = lambda n: jnp.zeros((length, n), _f32)
    tail = width - lead - dim
    c = jnp.concatenate([jnp.ones((length, lead), _f32), cos, cos, zeros(tail)], axis=1)
    s_up = jnp.concatenate([zeros(lead + half), sin, zeros(tail)], axis=1)
    s_dn = jnp.concatenate([zeros(lead), -sin, zeros(half + tail)], axis=1)
    return c, s_up, s_dn


def _apply_rope(x, c, s_up, s_dn, half):
    width = x.shape[-1]
    return x * c + pltpu.roll(x, half, 1) * s_up + pltpu.roll(x, width - half, 1) * s_dn


def _rms_rows(x, g):
    return x * lax.rsqrt(jnp.mean(x * x, axis=-1, keepdims=True) + RMS_EPS) * g


A_QW = 256


def _mla_q_kernel(x_ref, g_ref, w_ref, c_ref, su_ref, sd_ref, o_ref):
    xn = _rms_rows(x_ref[...], g_ref[...]).astype(_bf16)
    y = jnp.dot(xn, w_ref[...], preferred_element_type=_f32)
    c, su, sd = c_ref[...], su_ref[...], sd_ref[...]
    for h in range(A_HEADS):
        yh = _apply_rope(y[:, h * A_QW:(h + 1) * A_QW], c, su, sd, A_ROPE // 2)
        o_ref[0, h] = yh.astype(_bf16)


def _mla_kv_kernel(x_ref, kr_ref, g_ref, w_ref, c_ref, su_ref, sd_ref, k_ref, v_ref):
    xn = _rms_rows(x_ref[...], g_ref[...]).astype(_bf16)
    y = jnp.dot(xn, w_ref[...], preferred_element_type=_f32)
    kr = _apply_rope(kr_ref[...], c_ref[...], su_ref[...], sd_ref[...], A_ROPE // 2).astype(_bf16)
    hw = A_NOPE + A_VDIM
    for h in range(A_HEADS):
        k_ref[0, h] = jnp.concatenate([y[:, h * hw:h * hw + A_NOPE].astype(_bf16), kr], axis=1)
        v_ref[0, h] = y[:, h * hw + A_NOPE:(h + 1) * hw].astype(_bf16)


def _fold_lanes(x, op):
    r = x[:, 0:LANES]
    for n in range(1, x.shape[1] // LANES):
        r = op(r, x[:, n * LANES:(n + 1) * LANES])
    return r


def _lane_tile(x, width):
    return jnp.concatenate([x] * (width // LANES), axis=1)


def _for_blocks(n, fn):
    def pair(g, carry):
        fn([2 * g, 2 * g + 1])
        return carry

    lax.fori_loop(0, n // 2, pair, 0)

    @pl.when(n % 2 == 1)
    def _():
        fn([n - 1])


def _mla_flash_kernel(q_ref, k_ref, v_ref, o_ref, s_ref, mx_ref, l_ref, acc_ref, *, log2_scale, tile):
    i = pl.program_id(2)
    q = q_ref[0, 0]

    def scores(j):
        k = k_ref[0, 0, pl.ds(pl.multiple_of(j * tile, tile), tile), :]
        return lax.dot_general(q, k, (((1,), (1,)), ((), ())), preferred_element_type=_f32) * log2_scale

    def keep(js, ss):
        mx = mx_ref[...]
        for j, s in zip(js, ss):
            s_ref[j] = s
            mx = jnp.maximum(mx, _fold_lanes(s, jnp.maximum))
        mx_ref[...] = mx

    mx_ref[...] = jnp.full(mx_ref.shape, NEG_BIG, _f32)
    _for_blocks(i, lambda js: keep(js, [scores(j) for j in js]))
    row = lax.broadcasted_iota(jnp.int32, (tile, tile), 0) // CHUNK
    col = lax.broadcasted_iota(jnp.int32, (tile, tile), 1) // CHUNK
    keep([i], [jnp.where(col <= row, scores(i), NEG_BIG)])

    mx_ref[...] = jnp.broadcast_to(jnp.max(mx_ref[...], axis=1, keepdims=True), mx_ref.shape)
    l_ref[...] = jnp.zeros(l_ref.shape, _f32)
    acc_ref[...] = jnp.zeros(acc_ref.shape, _f32)

    def sweep2(js):
        m = _lane_tile(mx_ref[...], tile)
        l, acc = l_ref[...], acc_ref[...]
        for j in js:
            p = jnp.exp2(s_ref[j] - m)
            l += _fold_lanes(p, jnp.add)
            v = v_ref[0, 0, pl.ds(pl.multiple_of(j * tile, tile), tile), :]
            acc += jnp.dot(p.astype(_bf16), v, preferred_element_type=_f32)
        l_ref[...] = l
        acc_ref[...] = acc

    _for_blocks(i + 1, sweep2)
    o_ref[...] = (acc_ref[...] / jnp.sum(l_ref[...], axis=1, keepdims=True)).astype(o_ref.dtype)


def mla_mixer(proj, bsz, length, g_q, w_uq, g_kv, w_ukv):
    m = bsz * length
    tm = min(PROJ_TM, length)
    nt = length // tm
    heads = A_HEADS
    qd = A_NOPE + A_ROPE
    wq = jnp.pad(w_uq.reshape(A_Q_LORA, heads, qd), ((0, 0), (0, 0), (0, A_QW - qd)))
    wq = wq.reshape(A_Q_LORA, heads * A_QW).astype(_bf16)
    wkv = w_ukv.astype(_bf16)
    qc, qsu, qsd = _rope_tables(length, A_ROPE, A_NOPE, A_QW)
    kc, ksu, ksd = _rope_tables(length, A_ROPE, 0, LANES)

    tab_q = pl.BlockSpec((tm, A_QW), lambda i: (i % nt, 0))
    tab_k = pl.BlockSpec((tm, LANES), lambda i: (i % nt, 0))
    head_out = lambda w: pl.BlockSpec((1, heads, tm, w), lambda i: (i // nt, 0, i % nt, 0))
    whole = lambda a: pl.BlockSpec(a.shape, lambda i: (0,) * a.ndim)
    gq, gkv = g_q.reshape(1, -1), g_kv.reshape(1, -1)

    q = pl.pallas_call(
        _mla_q_kernel,
        grid=(m // tm,),
        in_specs=[pl.BlockSpec((tm, A_Q_LORA), lambda i: (i, COL_QLAT // A_Q_LORA)),
                  whole(gq), whole(wq), tab_q, tab_q, tab_q],
        out_specs=head_out(A_QW),
        out_shape=jax.ShapeDtypeStruct((bsz, heads, length, A_QW), _bf16),
        compiler_params=_params("parallel"),
        name="mla_q_proj",
    )(proj, gq, wq, qc, qsu, qsd)

    k, v = pl.pallas_call(
        _mla_kv_kernel,
        grid=(m // tm,),
        in_specs=[pl.BlockSpec((tm, A_KV_LORA), lambda i: (i, COL_KVLAT // A_KV_LORA)),
                  pl.BlockSpec((tm, LANES), lambda i: (i, COL_KROPE // LANES)),
                  whole(gkv), whole(wkv), tab_k, tab_k, tab_k],
        out_specs=(head_out(A_QW), head_out(A_VDIM)),
        out_shape=(jax.ShapeDtypeStruct((bsz, heads, length, A_QW), _bf16),
                   jax.ShapeDtypeStruct((bsz, heads, length, A_VDIM), _bf16)),
        compiler_params=_params("parallel"),
        name="mla_kv_proj",
    )(proj, proj, gkv, wkv, kc, ksu, ksd)

    t = min(ATT_T, length)
    nq = length // t
    return pl.pallas_call(
        functools.partial(_mla_flash_kernel, log2_scale=float(qd) ** -0.5 * LOG2E, tile=t),
        grid=(bsz, heads, nq),
        in_specs=[pl.BlockSpec((1, 1, t, A_QW), lambda b, h, i: (b, h, i, 0)),
                  pl.BlockSpec((1, 1, length, A_QW), lambda b, h, i: (b, h, 0, 0)),
                  pl.BlockSpec((1, 1, length, A_VDIM), lambda b, h, i: (b, h, 0, 0))],
        out_specs=pl.BlockSpec((t, A_VDIM), lambda b, h, i: (b * nq + i, h)),
        out_shape=jax.ShapeDtypeStruct((m, heads * A_VDIM), _bf16),
        scratch_shapes=[pltpu.VMEM((nq, t, t), _f32), pltpu.VMEM((t, LANES), _f32),
                        pltpu.VMEM((t, LANES), _f32), pltpu.VMEM((t, A_VDIM), _f32)],
        compiler_params=_params("parallel", "parallel", "arbitrary"),
        name="mla_flash",
    )(q, k, v)


def _s5_operators(lam_re, lam_im, log_step, b_re, b_im, c_re, c_im, d_skip):
    hp = lax.Precision.HIGHEST
    g, n, p, tc = S_GROUPS, S_STATE, S_GROUP_CH, S_TC
    gl = S_LB_GROUPS
    nlb = g // gl
    lr, li = lam_re.astype(_f32), lam_im.astype(_f32)
    dt = jnp.exp(log_step.astype(_f32))[:, None]
    steps = jnp.arange(tc + 1, dtype=_f32)[:, None, None]
    mag = jnp.exp(lr * dt * steps)
    pw_re, pw_im = mag * jnp.cos(li * dt * steps), mag * jnp.sin(li * dt * steps)
    ab_re, ab_im = pw_re[1], pw_im[1]
    den = lr * lr + li * li
    nr, ni = ab_re - 1.0, ab_im
    co_re = (nr * lr + ni * li) / den
    co_im = (ni * lr - nr * li) / den
    br, bi = b_re.astype(_f32), b_im.astype(_f32)
    bb_re = co_re[..., None] * br - co_im[..., None] * bi
    bb_im = co_re[..., None] * bi + co_im[..., None] * br
    cr, ci = c_re.astype(_f32), c_im.astype(_f32)
    cp_re = cr[None] * pw_re[:, :, None, :] - ci[None] * pw_im[:, :, None, :]
    cp_im = cr[None] * pw_im[:, :, None, :] + ci[None] * pw_re[:, :, None, :]
    kern = (jnp.einsum("dgpn,gnq->dgpq", cp_re, bb_re, precision=hp)
            - jnp.einsum("dgpn,gnq->dgpq", cp_im, bb_im, precision=hp))
    s_idx = jnp.arange(tc)[:, None]
    t_idx = jnp.arange(tc)[None, :]
    lag = t_idx - s_idx
    toe = kern[jnp.clip(lag, 0, tc)]
    toe = jnp.where((lag >= 0)[:, :, None, None, None], toe, 0.0)
    eye_p = (lag == 0)[:, :, None, None, None] * jnp.eye(p, dtype=_f32)[None, None, None]
    toe = toe + eye_p * d_skip.astype(_f32)[None, None, :, :, None]
    pr, pi_ = pw_re[tc - 1 - jnp.arange(tc)], pw_im[tc - 1 - jnp.arange(tc)]
    bs_re = pr[..., None] * bb_re[None] - pi_[..., None] * bb_im[None]
    bs_im = pr[..., None] * bb_im[None] + pi_[..., None] * bb_re[None]
    eye = jnp.eye(gl, dtype=_f32)
    m4 = jnp.einsum("stlgpq,gh->lsgqthp", toe.reshape(tc, tc, nlb, gl, p, p), eye)
    m4 = m4.reshape(nlb, tc * LANES, tc * LANES).astype(_bf16)

    def state_in(bs):
        r = jnp.einsum("slgnq,gh->lsgqhn", bs.reshape(tc, nlb, gl, n, p), eye)
        return r.reshape(nlb, tc * LANES, gl * n)

    def state_out(cpw):
        r = jnp.einsum("tlgpn,gh->lgnthp", cpw.reshape(tc, nlb, gl, p, n), eye)
        return r.reshape(nlb, gl * n, tc * LANES)

    b4 = jnp.concatenate([state_in(bs_re), state_in(bs_im)], axis=2).astype(_bf16)
    c4 = jnp.concatenate([state_out(cp_re[1:]), -state_out(cp_im[1:])], axis=1).astype(_bf16)
    a4 = jnp.stack([pw_re[tc].reshape(nlb, gl * n), pw_im[tc].reshape(nlb, gl * n)], axis=1)
    return m4, b4, c4, a4


def _gelu_tanh(x):
    return 0.5 * x * (1.0 + jnp.tanh(math.sqrt(2.0 / math.pi) * (x + 0.044715 * (x * x * x))))


def _s5_kernel(x_ref, m4_ref, b4_ref, c4_ref, a4_ref, o_ref, s_ref, hp_ref, *, nchunks):
    u = jnp.concatenate([x_ref[pl.ds(s, nchunks, stride=S_TC), :].astype(_bf16) for s in range(S_TC)], axis=1)
    s_ref[...] = jnp.dot(u, b4_ref[0], preferred_element_type=_f32)
    half = s_ref.shape[1] // 2
    a_re = a4_ref[0, 0:1, :]
    a_im = a4_ref[0, 1:2, :]

    def step(c8, carry):
        h_re, h_im = carry
        base = pl.multiple_of(c8 * SUBLANES, SUBLANES)
        s_blk = s_ref[pl.ds(base, SUBLANES), :]
        rows_re, rows_im = [], []
        for r in range(SUBLANES):
            rows_re.append(h_re)
            rows_im.append(h_im)
            s_re = s_blk[r:r + 1, 0:half]
            s_im = s_blk[r:r + 1, half:2 * half]
            h_re, h_im = a_re * h_re - a_im * h_im + s_re, a_re * h_im + a_im * h_re + s_im
        hp_ref[pl.ds(base, SUBLANES), :] = jnp.concatenate(
            [jnp.concatenate(rows_re, axis=0), jnp.concatenate(rows_im, axis=0)], axis=1)
        return h_re, h_im

    zero = jnp.zeros((1, half), _f32)
    lax.fori_loop(0, nchunks // SUBLANES, step, (zero, zero))
    y = jnp.dot(u, m4_ref[0], preferred_element_type=_f32)
    y += jnp.dot(hp_ref[...].astype(_bf16), c4_ref[0], preferred_element_type=_f32)
    g = _gelu_tanh(y)
    for t in range(S_TC):
        o_ref[pl.ds(t, nchunks, stride=S_TC), :] = g[:, t * LANES:(t + 1) * LANES]


def _glu_kernel(g_ref, w_ref, b_ref, o_ref):
    g = g_ref[...]
    z = jnp.dot(g.astype(_bf16), w_ref[...], preferred_element_type=_f32) + b_ref[...]
    o_ref[...] = (g * jax.nn.sigmoid(z)).astype(o_ref.dtype)


def s5_mixer(proj, bsz, length, lam_re, lam_im, log_step, b_re, b_im, c_re, c_im, d_skip, w_glu, b_glu):
    m = bsz * length
    nch = length // S_TC
    nlb = S_WIDTH // LANES
    wide = S_TC * LANES
    nstate = S_LB_GROUPS * S_STATE
    m4, b4, c4, a4 = _s5_operators(lam_re, lam_im, log_step, b_re, b_im, c_re, c_im, d_skip)
    once = dict(pipeline_mode=pl.Buffered(1))
    g = pl.pallas_call(
        functools.partial(_s5_kernel, nchunks=nch),
        grid=(nlb, bsz),
        in_specs=[pl.BlockSpec((length, LANES), lambda lb, b: (b, COL_U // LANES + lb)),
                  pl.BlockSpec((1, wide, wide), lambda lb, b: (lb, 0, 0), **once),
                  pl.BlockSpec((1, wide, 2 * nstate), lambda lb, b: (lb, 0, 0), **once),
                  pl.BlockSpec((1, 2 * nstate, wide), lambda lb, b: (lb, 0, 0), **once),
                  pl.BlockSpec((1, 2, nstate), lambda lb, b: (lb, 0, 0))],
        out_specs=pl.BlockSpec((length, LANES), lambda lb, b: (b, lb)),
        out_shape=jax.ShapeDtypeStruct((m, S_WIDTH), _f32),
        scratch_shapes=[pltpu.VMEM((nch, 2 * nstate), _f32), pltpu.VMEM((nch, 2 * nstate), _f32)],
        compiler_params=_params("arbitrary", "arbitrary"),
        name="s5_scan",
    )(proj, m4, b4, c4, a4)
    tm = min(MM_TM, m)
    return pl.pallas_call(
        _glu_kernel,
        grid=(m // tm,),
        in_specs=[pl.BlockSpec((tm, S_WIDTH), lambda i: (i, 0)),
                  pl.BlockSpec((S_WIDTH, S_WIDTH), lambda i: (0, 0)),
                  pl.BlockSpec((1, S_WIDTH), lambda i: (0, 0))],
        out_specs=pl.BlockSpec((tm, S_WIDTH), lambda i: (i, 0)),
        out_shape=jax.ShapeDtypeStruct((m, S_WIDTH), _bf16),
        compiler_params=_params("parallel"),
        name="s5_glu",
    )(g, w_glu.astype(_bf16), b_glu.reshape(1, -1).astype(_f32))


def _t5_bucket(rel):
    half = T5_BUCKETS // 2
    max_exact = half // 2
    ret = jnp.where(rel > 0, half, 0)
    n = jnp.abs(rel)
    nf = jnp.maximum(n, 1).astype(_f32)
    large = max_exact + (jnp.log(nf / max_exact) / math.log(T5_MAX_DIST / max_exact)
                         * (half - max_exact)).astype(jnp.int32)
    large = jnp.minimum(large, half - 1)
    return ret + jnp.where(n < max_exact, n, large)


def _dsa_bias_tiles(rel_bias, t):
    assert t >= T5_MAX_DIST
    hp = lax.Precision.HIGHEST
    r = jnp.arange(t, dtype=jnp.int32)[:, None]
    c = jnp.arange(t, dtype=jnp.int32)[None, :]
    rel = jnp.stack([c - r - t, c - r])
    onehot = (_t5_bucket(rel)[..., None] == jnp.arange(T5_BUCKETS)).astype(_f32)
    far = rel_bias.astype(_f32)[T5_BUCKETS // 2 - 1]
    tiles = jnp.einsum("xrcb,bh->hxrc", onehot, rel_bias.astype(_f32) - far[None, :], precision=hp)
    return tiles * LOG2E


def _dsa_prep_kernel(qi_ref, ki_ref, qc_ref, qsu_ref, qsd_ref, kc_ref, ksu_ref, ksd_ref, qo_ref, ko_ref):
    half = I_ROPE // 2
    rep = (I_HEADS * I_DIM) // LANES
    tile = lambda r: jnp.concatenate([r[...]] * rep, axis=1)
    q = _apply_rope(qi_ref[...], tile(qc_ref), tile(qsu_ref), tile(qsd_ref), half)
    for h in range(I_HEADS):
        qo_ref[0, h] = q[:, h * I_DIM:(h + 1) * I_DIM].astype(_bf16)
    ko_ref[...] = _apply_rope(ki_ref[...], kc_ref[...], ksu_ref[...], ksd_ref[...], half).astype(_bf16)


def _float_key(x):
    b = lax.bitcast_convert_type(x, jnp.int32)
    return b ^ ((b >> 31) & 0x7FFFFFFF)


def _dsa_kernel(qi_ref, wi_ref, kit_ref, qc_ref, kct_ref, vc_ref, bias_ref, o_ref,
                keys_ref, s_ref, wb_ref, qb_ref, mx_ref, l_ref, acc_ref, *, tile, k_sel, log2_scale, idx_scale):
    i = pl.program_id(1)

    w = wi_ref[...]
    for h in range(I_HEADS):
        wb_ref[h] = jnp.broadcast_to(w[:, h:h + 1], (tile, LANES))
    qf = qc_ref[...]
    for h in range(C_HEADS):
        qb_ref[h] = qf[:, h * C_HEAD_DIM:(h + 1) * C_HEAD_DIM].astype(_bf16)

    row_chunk = lax.broadcasted_iota(jnp.int32, (tile, tile), 0) // CHUNK
    col_chunk = lax.broadcasted_iota(jnp.int32, (tile, tile), 1) // CHUNK
    admissible_diag = col_chunk <= row_chunk

    def score_block(j, carry):
        kt = kit_ref[0, j]
        acc = jnp.zeros((tile, tile), _f32)
        for h in range(I_HEADS):
            d = jnp.dot(qi_ref[0, h], kt, preferred_element_type=_f32)
            acc += jnp.maximum(d, 0.0) * _lane_tile(wb_ref[h], tile)
        key = _float_key(acc * idx_scale)
        keys_ref[j] = jnp.where(jnp.logical_or(j < i, admissible_diag), key, INT_MIN)
        return carry

    lax.fori_loop(0, i + 1, score_block, 0)

    def count_ge(t):
        tb = jnp.broadcast_to(t, (tile, LANES))

        def blk(j, c):
            kj = keys_ref[j]
            for s in range(tile // LANES):
                c += jnp.where(kj[:, s * LANES:(s + 1) * LANES] >= tb, 1, 0)
            return c

        c = lax.fori_loop(0, i + 1, blk, jnp.zeros((tile, LANES), jnp.int32))
        return jnp.sum(c, axis=1, keepdims=True)

    def search_cond(carry):
        bit, _, _, settled = carry
        return jnp.logical_and(bit < 32, settled == 0)

    def search_step(carry):
        bit, t, cnt_t, _ = carry
        cand = t + jnp.left_shift(jnp.int32(1), 31 - bit)
        cnt = count_ge(cand)
        take = cnt >= k_sel
        t = jnp.where(take, cand, t)
        cnt_t = jnp.where(take, cnt, cnt_t)
        settled = jnp.min(jnp.where(cnt_t == k_sel, 1, 0))
        return bit + 1, t, cnt_t, settled

    all_selected = ((i + 1) * tile <= k_sel).astype(jnp.int32)
    _, thr, _, _ = lax.while_loop(
        search_cond, search_step,
        (jnp.int32(0), jnp.full((tile, 1), INT_MIN, jnp.int32), jnp.full((tile, 1), -1, jnp.int32), all_selected))
    thr = jnp.maximum(thr, INT_MIN + 1)
    thr_b = jnp.broadcast_to(thr, (tile, tile))

    def mask_block(j, carry):
        mb = jnp.where(keys_ref[j] >= thr_b, 0.0, NEG_BIG).astype(_f32)
        keys_ref[j] = lax.bitcast_convert_type(mb, jnp.int32)
        return carry

    lax.fori_loop(0, i + 1, mask_block, 0)

    grp = DSA_HEAD_GROUP
    rows = grp * tile

    def group_body(g, carry):
        h0 = pl.multiple_of(g * grp, grp)
        q = qb_ref[pl.ds(h0, grp)].reshape(rows, C_HEAD_DIM)
        mx_ref[...] = jnp.full(mx_ref.shape, NEG_BIG, _f32)

        def scores(j):
            s = jnp.dot(q, kct_ref[0, j], preferred_element_type=_f32) * log2_scale
            return s.reshape(grp, tile, tile) + lax.bitcast_convert_type(keys_ref[j], _f32)[None]

        def keep(js, ss):
            mx = mx_ref[...]
            for j, s in zip(js, ss):
                s_ref[:, j] = s
                mx = jnp.maximum(mx, _fold_lanes(s.reshape(rows, tile), jnp.maximum))
            mx_ref[...] = mx

        _for_blocks(jnp.maximum(i - 1, 0), lambda js: keep(js, [scores(j) for j in js]))

        @pl.when(i >= 1)
        def _():
            keep([i - 1], [scores(i - 1) + bias_ref[pl.ds(h0, grp), 0]])

        keep([i], [scores(i) + bias_ref[pl.ds(h0, grp), 1]])

        mx_ref[...] = jnp.broadcast_to(jnp.max(mx_ref[...], axis=1, keepdims=True), mx_ref.shape)
        l_ref[...] = jnp.zeros(l_ref.shape, _f32)
        acc_ref[pl.ds(h0, grp)] = jnp.zeros((grp, tile, C_HEAD_DIM), _f32)

        def sweep2(js):
            m = _lane_tile(mx_ref[...], tile)
            l, acc = l_ref[...], acc_ref[pl.ds(h0, grp)]
            for j in js:
                p = jnp.exp2(s_ref[:, j].reshape(rows, tile) - m)
                l += _fold_lanes(p, jnp.add)
                pv = jnp.dot(p.astype(_bf16), vc_ref[0, j], preferred_element_type=_f32)
                acc += pv.reshape(grp, tile, C_HEAD_DIM)
            l_ref[...] = l
            acc_ref[pl.ds(h0, grp)] = acc

        _for_blocks(i + 1, sweep2)
        denom = jnp.sum(l_ref[...], axis=1, keepdims=True).reshape(grp, tile, 1)
        acc_ref[pl.ds(h0, grp)] = acc_ref[pl.ds(h0, grp)] / denom
        return carry

    lax.fori_loop(0, C_HEADS // grp, group_body, 0)
    for h in range(C_HEADS):
        o_ref[:, h * C_HEAD_DIM:(h + 1) * C_HEAD_DIM] = acc_ref[h].astype(o_ref.dtype)


def dsa_mixer(proj, bsz, length, rel_bias):
    m = bsz * length
    t = min(DSA_T, length)
    nb = length // t
    k_sel = min(TOPK_MAX, length // 4)
    tm = min(PROJ_TM, length)
    nt = length // tm
    iw = I_HEADS * I_DIM
    qtab = _rope_tables(length, I_ROPE, 0, I_DIM)
    qtab = [jnp.concatenate([a.at[:, I_ROPE:].set(1.0) if n == 0 else a] * (LANES // I_DIM), axis=1)
            for n, a in enumerate(qtab)]
    ktab = _rope_tables(length, I_ROPE, 0, LANES)
    ktab = [ktab[0].at[:, I_ROPE:I_DIM].set(1.0), ktab[1], ktab[2]]
    tab = pl.BlockSpec((tm, LANES), lambda i: (i % nt, 0))
    qi, ki = pl.pallas_call(
        _dsa_prep_kernel,
        grid=(m // tm,),
        in_specs=[pl.BlockSpec((tm, iw), lambda i: (i, COL_QI // iw)),
                  pl.BlockSpec((tm, LANES), lambda i: (i, COL_KI // LANES)),
                  tab, tab, tab, tab, tab, tab],
        out_specs=(pl.BlockSpec((1, I_HEADS, tm, I_DIM), lambda i: (i // nt, 0, i % nt, 0)),
                   pl.BlockSpec((tm, LANES), lambda i: (i, 0))),
        out_shape=(jax.ShapeDtypeStruct((bsz, I_HEADS, length, I_DIM), _bf16),
                   jax.ShapeDtypeStruct((m, LANES), _bf16)),
        compiler_params=_params("parallel"),
        name="dsa_prep",
    )(proj, proj, *qtab, *ktab)

    kit = jnp.transpose(ki[:, :I_DIM].reshape(bsz, nb, t, I_DIM), (0, 1, 3, 2))
    kct = jnp.transpose(proj[:, COL_KC:COL_KC + C_HEAD_DIM].astype(_bf16).reshape(bsz, nb, t, C_HEAD_DIM),
                        (0, 1, 3, 2))
    vc = proj[:, COL_VC:COL_VC + C_HEAD_DIM].astype(_bf16).reshape(bsz, nb, t, C_HEAD_DIM)
    bias = _dsa_bias_tiles(rel_bias, t)
    qw = C_HEADS * C_HEAD_DIM
    whole = lambda shape: pl.BlockSpec((1,) + shape, lambda b, i: (b, 0, 0, 0))
    return pl.pallas_call(
        functools.partial(_dsa_kernel, tile=t, k_sel=k_sel, log2_scale=float(C_HEAD_DIM) ** -0.5 * LOG2E,
                          idx_scale=float(I_DIM * I_HEADS) ** -0.5),
        grid=(bsz, nb),
        in_specs=[pl.BlockSpec((1, I_HEADS, t, I_DIM), lambda b, i: (b, 0, i, 0)),
                  pl.BlockSpec((t, LANES), lambda b, i: (b * nb + i, COL_WI // LANES)),
                  whole((nb, I_DIM, t)),
                  pl.BlockSpec((t, qw), lambda b, i: (b * nb + i, COL_QC // qw)),
                  whole((nb, C_HEAD_DIM, t)),
                  whole((nb, t, C_HEAD_DIM)),
                  pl.BlockSpec((C_HEADS, 2, t, t), lambda b, i: (0, 0, 0, 0),
                               pipeline_mode=pl.Buffered(1))],
        out_specs=pl.BlockSpec((t, qw), lambda b, i: (b * nb + i, 0)),
        out_shape=jax.ShapeDtypeStruct((m, qw), _bf16),
        scratch_shapes=[pltpu.VMEM((nb, t, t), jnp.int32),
                        pltpu.VMEM((DSA_HEAD_GROUP, nb, t, t), _f32),
                        pltpu.VMEM((I_HEADS, t, LANES), _f32),
                        pltpu.VMEM((C_HEADS, t, C_HEAD_DIM), _bf16),
                        pltpu.VMEM((DSA_HEAD_GROUP * t, LANES), _f32),
                        pltpu.VMEM((DSA_HEAD_GROUP * t, LANES), _f32),
                        pltpu.VMEM((C_HEADS, t, C_HEAD_DIM), _f32)],
        compiler_params=_params("parallel", "arbitrary"),
        name="dsa_attention",
    )(qi, proj, kit, proj, kct, vc, bias)


def _reorder_w_in(w):
    sizes = (A_Q_LORA, A_KV_LORA, A_ROPE, S_WIDTH, C_HEADS * C_HEAD_DIM, C_HEAD_DIM, C_HEAD_DIM,
             I_HEADS * I_DIM, I_DIM, I_HEADS)
    offs = [0]
    for s in sizes:
        offs.append(offs[-1] + s)
    q_lat, kv_lat, k_rope, u, q_c, k_c, v_c, q_i, k_i, w_i = [w[:, offs[n]:offs[n + 1]] for n in range(10)]
    pad = lambda a, width: jnp.pad(a, ((0, 0), (0, width - a.shape[1])))
    cols = [q_c, q_lat, k_c, v_c, pad(k_rope, LANES), pad(k_i, LANES), pad(w_i, LANES),
            jnp.zeros((w.shape[0], LANES), w.dtype), u, q_i, kv_lat]
    out = jnp.concatenate(cols, axis=1).astype(_bf16)
    assert out.shape[1] == PROJ_WIDTH
    return out


def kernel(x, ln_in_g, ln_in_b, rel_bias, w_in, a_gq, a_wuq, a_gkv, a_wukv, s_lam_re, s_lam_im, s_log_step, s_b_re, s_b_im, s_c_re, s_c_im, s_d, s_w_glu, s_b_glu, w_out, ln1_g, ln1_b, w_ff1, w_ff2, ln2_g, ln2_b):
    bsz, length, d_model = x.shape
    depth = w_in.shape[0]
    alpha = (2 * depth) ** 0.25
    m = bsz * length

    h, hb = layer_norm(x.reshape(m, d_model), ln_in_g, ln_in_b)
    for l in range(depth):
        proj = matmul(hb, _reorder_w_in(w_in[l]), _f32)
        a_out = mla_mixer(proj, bsz, length, a_gq[l], a_wuq[l], a_gkv[l], a_wukv[l])
        b_out = s5_mixer(proj, bsz, length, s_lam_re[l], s_lam_im[l], s_log_step[l], s_b_re[l], s_b_im[l],
                         s_c_re[l], s_c_im[l], s_d[l], s_w_glu[l], s_b_glu[l])
        c_out = dsa_mixer(proj, bsz, length, rel_bias)
        mix = matmul_concat3(a_out, b_out, c_out, w_out[l].astype(_bf16))
        h, hb = layer_norm(mix, ln1_g[l], ln1_b[l], resid=h, alpha=alpha)
        act = matmul(hb, w_ff1[l].astype(_bf16), _bf16, act="relu2")
        ff = matmul_ksplit(act, w_ff2[l].astype(_bf16))
        h, hb = layer_norm(ff, ln2_g[l], ln2_b[l], resid=h, alpha=alpha)
    return h.reshape(bsz, length, d_model)
```

```python
import functools
import math

import jax
import jax.numpy as jnp
from jax import lax
from jax.experimental import pallas as pl
from jax.experimental.pallas import tpu as pltpu

CHUNK = 64
A_HEADS, A_NOPE, A_ROPE, A_VDIM = 12, 128, 64, 128
A_Q_LORA, A_KV_LORA = 768, 512
S_GROUP_CH, S_WIDTH, S_STATE = 16, 1024, 64
S_GROUPS = S_WIDTH // S_GROUP_CH
C_HEADS, C_HEAD_DIM = 12, 128
I_HEADS, I_DIM, I_ROPE = 16, 64, 32
TOPK_MAX = 256
T5_BUCKETS, T5_MAX_DIST = 32, 128
ROPE_THETA = 10000.0
LN_EPS = 1e-5
RMS_EPS = 1e-6

LANES = 128
SUBLANES = 8
VMEM_LIMIT_BYTES = 56 * 1024 * 1024

MM_TM, MM_TN, MM_TK = 1024, 1024, 4096
ROW_TILE = 256
PROJ_TM = 512
ATT_T = 512
DSA_T = 256
DSA_HEAD_GROUP = 4
S_TC = 16
S_LB_GROUPS = LANES // S_GROUP_CH

COL_QC, COL_QLAT, COL_KC, COL_VC, COL_KROPE, COL_KI, COL_WI = 0, 1536, 2304, 2432, 2560, 2688, 2816
COL_U, COL_QI, COL_KVLAT, PROJ_WIDTH = 3072, 4096, 5120, 5632

NEG_BIG = -1e30
LOG2E = math.log2(math.e)
INT_MIN = -2 ** 31

_bf16 = jnp.bfloat16
_f32 = jnp.float32


def _tile(n, pref):
    t = min(pref, n)
    while n % t:
        t //= 2
    return t


def _params(*sem):
    return pltpu.CompilerParams(dimension_semantics=sem, vmem_limit_bytes=VMEM_LIMIT_BYTES)


def _mm_kernel(x_ref, w_ref, o_ref, *, act):
    y = jnp.dot(x_ref[...], w_ref[...], preferred_element_type=_f32)
    if act == "relu2":
        y = jnp.square(jnp.maximum(y, 0.0))
    o_ref[...] = y.astype(o_ref.dtype)


def matmul(x, w, out_dtype, act=None):
    m, k = x.shape
    n = w.shape[1]
    tm, tn = _tile(m, MM_TM), _tile(n, MM_TN)
    return pl.pallas_call(
        functools.partial(_mm_kernel, act=act),
        grid=(m // tm, n // tn),
        in_specs=[pl.BlockSpec((tm, k), lambda i, j: (i, 0)),
                  pl.BlockSpec((k, tn), lambda i, j: (0, j))],
        out_specs=pl.BlockSpec((tm, tn), lambda i, j: (i, j)),
        out_shape=jax.ShapeDtypeStruct((m, n), out_dtype),
        compiler_params=_params("parallel", "parallel"),
        name="matmul",
    )(x, w)


def _mm_acc_kernel(x_ref, w_ref, o_ref):
    y = jnp.dot(x_ref[...], w_ref[...], preferred_element_type=_f32)

    @pl.when(pl.program_id(2) == 0)
    def _():
        o_ref[...] = y

    @pl.when(pl.program_id(2) != 0)
    def _():
        o_ref[...] += y


def matmul_ksplit(x, w):
    m, k = x.shape
    n = w.shape[1]
    tm, tn, tk = min(MM_TM, m), min(MM_TN, n), min(MM_TK, k)
    return pl.pallas_call(
        _mm_acc_kernel,
        grid=(m // tm, n // tn, k // tk),
        in_specs=[pl.BlockSpec((tm, tk), lambda i, j, l: (i, l)),
                  pl.BlockSpec((tk, tn), lambda i, j, l: (l, j))],
        out_specs=pl.BlockSpec((tm, tn), lambda i, j, l: (i, j)),
        out_shape=jax.ShapeDtypeStruct((m, n), _f32),
        compiler_params=_params("parallel", "parallel", "arbitrary"),
        name="matmul_ksplit",
    )(x, w)


def _mm3_kernel(a_ref, b_ref, c_ref, w_ref, o_ref):
    ka, kb = a_ref.shape[1], b_ref.shape[1]
    y = jnp.dot(a_ref[...], w_ref[0:ka, :], preferred_element_type=_f32)
    y += jnp.dot(b_ref[...], w_ref[ka:ka + kb, :], preferred_element_type=_f32)
    y += jnp.dot(c_ref[...], w_ref[ka + kb:, :], preferred_element_type=_f32)
    o_ref[...] = y


def matmul_concat3(a, b, c, w):
    m = a.shape[0]
    k, n = w.shape
    assert a.shape[1] + b.shape[1] + c.shape[1] == k
    tm, tn = _tile(m, MM_TM), _tile(n, MM_TN)
    row = lambda i, j: (i, 0)
    return pl.pallas_call(
        _mm3_kernel,
        grid=(m // tm, n // tn),
        in_specs=[pl.BlockSpec((tm, a.shape[1]), row), pl.BlockSpec((tm, b.shape[1]), row),
                  pl.BlockSpec((tm, c.shape[1]), row),
                  pl.BlockSpec((k, tn), lambda i, j: (0, j))],
        out_specs=pl.BlockSpec((tm, tn), lambda i, j: (i, j)),
        out_shape=jax.ShapeDtypeStruct((m, n), _f32),
        compiler_params=_params("parallel", "parallel"),
        name="matmul_concat3",
    )(a, b, c, w)


def _ln_rows(z, g, b):
    mu = jnp.mean(z, axis=-1, keepdims=True)
    zc = z - mu
    var = jnp.mean(zc * zc, axis=-1, keepdims=True)
    return zc * lax.rsqrt(var + LN_EPS) * g + b


def _ln_kernel(x_ref, g_ref, b_ref, o_ref, ob_ref):
    y = _ln_rows(x_ref[...], g_ref[...], b_ref[...])
    o_ref[...] = y
    ob_ref[...] = y.astype(_bf16)


def _res_ln_kernel(h_ref, y_ref, g_ref, b_ref, o_ref, ob_ref, *, alpha):
    y = _ln_rows(alpha * h_ref[...] + y_ref[...], g_ref[...], b_ref[...])
    o_ref[...] = y
    ob_ref[...] = y.astype(_bf16)


def layer_norm(x, g, b, resid=None, alpha=1.0):
    m, d = x.shape
    tr = min(ROW_TILE, m)
    rows = pl.BlockSpec((tr, d), lambda i: (i, 0))
    vec = pl.BlockSpec((1, d), lambda i: (0, 0))
    out_shape = (jax.ShapeDtypeStruct((m, d), _f32), jax.ShapeDtypeStruct((m, d), _bf16))
    g2, b2 = g.reshape(1, d), b.reshape(1, d)
    if resid is None:
        return pl.pallas_call(_ln_kernel, grid=(m // tr,), in_specs=[rows, vec, vec],
                              out_specs=(rows, rows), out_shape=out_shape,
                              compiler_params=_params("parallel"), name="layer_norm")(x, g2, b2)
    return pl.pallas_call(functools.partial(_res_ln_kernel, alpha=alpha), grid=(m // tr,),
                          in_specs=[rows, rows, vec, vec], out_specs=(rows, rows),
                          out_shape=out_shape, compiler_params=_params("parallel"),
                          name="residual_layer_norm")(resid, x, g2, b2)


def _rope_cos_sin(length, dim):
    inv = ROPE_THETA ** (-jnp.arange(0, dim, 2, dtype=_f32) / dim)
    ang = jnp.arange(length, dtype=_f32)[:, None] * inv[None, :]
    return jnp.cos(ang), jnp.sin(ang)


def _rope_tables(length, dim, lead, width):
    cos, sin = _rope_cos_sin(length, dim)
    half = dim // 2
    zeros = lambda n: jnp.zeros((length, n), _f32)
    tail = width - lead - dim
    c = jnp.concatenate([jnp.ones((length, lead), _f32), cos, cos, zeros(tail)], axis=1)
    s_up = jnp.concatenate([zeros(lead + half), sin, zeros(tail)], axis=1)
    s_dn = jnp.concatenate([zeros(lead), -sin, zeros(half + tail)], axis=1)
    return c, s_up, s_dn


def _apply_rope(x, c, s_up, s_dn, half):
    width = x.shape[-1]
    return x * c + pltpu.roll(x, half, 1) * s_up + pltpu.roll(x, width - half, 1) * s_dn


def _rms_rows(x, g):
    return x * lax.rsqrt(jnp.mean(x * x, axis=-1, keepdims=True) + RMS_EPS) * g


A_QW = 256


def _mla_q_kernel(x_ref, g_ref, w_ref, c_ref, su_ref, sd_ref, o_ref):
    xn = _rms_rows(x_ref[...], g_ref[...]).astype(_bf16)
    y = jnp.dot(xn, w_ref[...], preferred_element_type=_f32)
    c, su, sd = c_ref[...], su_ref[...], sd_ref[...]
    for h in range(A_HEADS):
        yh = _apply_rope(y[:, h * A_QW:(h + 1) * A_QW], c, su, sd, A_ROPE // 2)
        o_ref[0, h] = yh.astype(_bf16)


def _mla_kv_kernel(x_ref, kr_ref, g_ref, w_ref, c_ref, su_ref, sd_ref, k_ref, v_ref):
    xn = _rms_rows(x_ref[...], g_ref[...]).astype(_bf16)
    y = jnp.dot(xn, w_ref[...], preferred_element_type=_f32)
    kr = _apply_rope(kr_ref[...], c_ref[...], su_ref[...], sd_ref[...], A_ROPE // 2).astype(_bf16)
    hw = A_NOPE + A_VDIM
    for h in range(A_HEADS):
        k_ref[0, h] = jnp.concatenate([y[:, h * hw:h * hw + A_NOPE].astype(_bf16), kr], axis=1)
        v_ref[0, h] = y[:, h * hw + A_NOPE:(h + 1) * hw].astype(_bf16)


def _fold_lanes(x, op):
    r = x[:, 0:LANES]
    for n in range(1, x.shape[1] // LANES):
        r = op(r, x[:, n * LANES:(n + 1) * LANES])
    return r


def _lane_tile(x, width):
    return jnp.concatenate([x] * (width // LANES), axis=1)


def _for_blocks(n, fn):
    def quad(g, carry):
        fn([4 * g + u for u in range(4)])
        return carry

    lax.fori_loop(0, n // 4, quad, 0)
    base = (n // 4) * 4

    @pl.when(n % 4 >= 2)
    def _():
        fn([base, base + 1])

    @pl.when(n % 2 == 1)
    def _():
        fn([n - 1])


def _mla_flash_kernel(q_ref, k_ref, v_ref, o_ref, s_ref, mx_ref, l_ref, acc_ref, *, log2_scale, tile):
    i = pl.program_id(2)
    q = q_ref[0, 0]

    def scores(j):
        k = k_ref[0, 0, pl.ds(pl.multiple_of(j * tile, tile), tile), :]
        return lax.dot_general(q, k, (((1,), (1,)), ((), ())), preferred_element_type=_f32) * log2_scale

    def keep(js, ss):
        mx = mx_ref[...]
        for j, s in zip(js, ss):
            s_ref[j] = s
            mx = jnp.maximum(mx, _fold_lanes(s, jnp.maximum))
        mx_ref[...] = mx

    mx_ref[...] = jnp.full(mx_ref.shape, NEG_BIG, _f32)
    _for_blocks(i, lambda js: keep(js, [scores(j) for j in js]))
    row = lax.broadcasted_iota(jnp.int32, (tile, tile), 0) // CHUNK
    col = lax.broadcasted_iota(jnp.int32, (tile, tile), 1) // CHUNK
    keep([i], [jnp.where(col <= row, scores(i), NEG_BIG)])

    mx_ref[...] = jnp.broadcast_to(jnp.max(mx_ref[...], axis=1, keepdims=True), mx_ref.shape)
    l_ref[...] = jnp.zeros(l_ref.shape, _f32)
    acc_ref[...] = jnp.zeros(acc_ref.shape, _f32)

    def sweep2(js):
        m = _lane_tile(mx_ref[...], tile)
        l, acc = l_ref[...], acc_ref[...]
        for j in js:
            p = jnp.exp2(s_ref[j] - m)
            l += _fold_lanes(p, jnp.add)
            v = v_ref[0, 0, pl.ds(pl.multiple_of(j * tile, tile), tile), :]
            acc += jnp.dot(p.astype(_bf16), v, preferred_element_type=_f32)
        l_ref[...] = l
        acc_ref[...] = acc

    _for_blocks(i + 1, sweep2)
    o_ref[...] = (acc_ref[...] / jnp.sum(l_ref[...], axis=1, keepdims=True)).astype(o_ref.dtype)


def mla_mixer(proj, bsz, length, g_q, w_uq, g_kv, w_ukv):
    m = bsz * length
    tm = min(PROJ_TM, length)
    nt = length // tm
    heads = A_HEADS
    qd = A_NOPE + A_ROPE
    wq = jnp.pad(w_uq.reshape(A_Q_LORA, heads, qd), ((0, 0), (0, 0), (0, A_QW - qd)))
    wq = wq.reshape(A_Q_LORA, heads * A_QW).astype(_bf16)
    wkv = w_ukv.astype(_bf16)
    qc, qsu, qsd = _rope_tables(length, A_ROPE, A_NOPE, A_QW)
    kc, ksu, ksd = _rope_tables(length, A_ROPE, 0, LANES)

    tab_q = pl.BlockSpec((tm, A_QW), lambda i: (i % nt, 0))
    tab_k = pl.BlockSpec((tm, LANES), lambda i: (i % nt, 0))
    head_out = lambda w: pl.BlockSpec((1, heads, tm, w), lambda i: (i // nt, 0, i % nt, 0))
    whole = lambda a: pl.BlockSpec(a.shape, lambda i: (0,) * a.ndim)
    gq, gkv = g_q.reshape(1, -1), g_kv.reshape(1, -1)

    q = pl.pallas_call(
        _mla_q_kernel,
        grid=(m // tm,),
        in_specs=[pl.BlockSpec((tm, A_Q_LORA), lambda i: (i, COL_QLAT // A_Q_LORA)),
                  whole(gq), whole(wq), tab_q, tab_q, tab_q],
        out_specs=head_out(A_QW),
        out_shape=jax.ShapeDtypeStruct((bsz, heads, length, A_QW), _bf16),
        compiler_params=_params("parallel"),
        name="mla_q_proj",
    )(proj, gq, wq, qc, qsu, qsd)

    k, v = pl.pallas_call(
        _mla_kv_kernel,
        grid=(m // tm,),
        in_specs=[pl.BlockSpec((tm, A_KV_LORA), lambda i: (i, COL_KVLAT // A_KV_LORA)),
                  pl.BlockSpec((tm, LANES), lambda i: (i, COL_KROPE // LANES)),
                  whole(gkv), whole(wkv), tab_k, tab_k, tab_k],
        out_specs=(head_out(A_QW), head_out(A_VDIM)),
        out_shape=(jax.ShapeDtypeStruct((bsz, heads, length, A_QW), _bf16),
                   jax.ShapeDtypeStruct((bsz, heads, length, A_VDIM), _bf16)),
        compiler_params=_params("parallel"),
        name="mla_kv_proj",
    )(proj, proj, gkv, wkv, kc, ksu, ksd)

    t = min(ATT_T, length)
    nq = length // t
    return pl.pallas_call(
        functools.partial(_mla_flash_kernel, log2_scale=float(qd) ** -0.5 * LOG2E, tile=t),
        grid=(bsz, heads, nq),
        in_specs=[pl.BlockSpec((1, 1, t, A_QW), lambda b, h, i: (b, h, i, 0)),
                  pl.BlockSpec((1, 1, length, A_QW), lambda b, h, i: (b, h, 0, 0)),
                  pl.BlockSpec((1, 1, length, A_VDIM), lambda b, h, i: (b, h, 0, 0))],
        out_specs=pl.BlockSpec((t, A_VDIM), lambda b, h, i: (b * nq + i, h)),
        out_shape=jax.ShapeDtypeStruct((m, heads * A_VDIM), _bf16),
        scratch_shapes=[pltpu.VMEM((nq, t, t), _f32), pltpu.VMEM((t, LANES), _f32),
                        pltpu.VMEM((t, LANES), _f32), pltpu.VMEM((t, A_VDIM), _f32)],
        compiler_params=_params("parallel", "parallel", "arbitrary"),
        name="mla_flash",
    )(q, k, v)


def _s5_operators(lam_re, lam_im, log_step, b_re, b_im, c_re, c_im, d_skip):
    hp = lax.Precision.HIGHEST
    g, n, p, tc = S_GROUPS, S_STATE, S_GROUP_CH, S_TC
    gl = S_LB_GROUPS
    nlb = g // gl
    lr, li = lam_re.astype(_f32), lam_im.astype(_f32)
    dt = jnp.exp(log_step.astype(_f32))[:, None]
    steps = jnp.arange(tc + 1, dtype=_f32)[:, None, None]
    mag = jnp.exp(lr * dt * steps)
    pw_re, pw_im = mag * jnp.cos(li * dt * steps), mag * jnp.sin(li * dt * steps)
    ab_re, ab_im = pw_re[1], pw_im[1]
    den = lr * lr + li * li
    nr, ni = ab_re - 1.0, ab_im
    co_re = (nr * lr + ni * li) / den
    co_im = (ni * lr - nr * li) / den
    br, bi = b_re.astype(_f32), b_im.astype(_f32)
    bb_re = co_re[..., None] * br - co_im[..., None] * bi
    bb_im = co_re[..., None] * bi + co_im[..., None] * br
    cr, ci = c_re.astype(_f32), c_im.astype(_f32)
    cp_re = cr[None] * pw_re[:, :, None, :] - ci[None] * pw_im[:, :, None, :]
    cp_im = cr[None] * pw_im[:, :, None, :] + ci[None] * pw_re[:, :, None, :]
    kern = (jnp.einsum("dgpn,gnq->dgpq", cp_re, bb_re, precision=hp)
            - jnp.einsum("dgpn,gnq->dgpq", cp_im, bb_im, precision=hp))
    pr, pi_ = pw_re[tc - 1 - jnp.arange(tc)], pw_im[tc - 1 - jnp.arange(tc)]
    bs_re = pr[..., None] * bb_re[None] - pi_[..., None] * bb_im[None]
    bs_im = pr[..., None] * bb_im[None] + pi_[..., None] * bb_re[None]
    eye = jnp.eye(gl, dtype=_f32)
    kern = kern.at[0].add(jnp.eye(p, dtype=_f32)[None] * d_skip.astype(_f32)[:, :, None])
    kt = jnp.transpose(kern[:tc].reshape(tc, nlb, gl, p, p), (1, 0, 2, 4, 3))
    bd = (kt[:, :, :, :, None, :] * eye[None, None, :, None, :, None]).reshape(nlb, tc, LANES, LANES)

    def state_in(bs):
        r = jnp.transpose(bs.reshape(tc, nlb, gl, n, p), (1, 0, 4, 2, 3))
        return r.reshape(nlb, tc * p, gl * n)

    def state_out(cpw):
        r = jnp.transpose(cpw.reshape(tc, nlb, gl, p, n), (1, 2, 4, 0, 3))
        return r.reshape(nlb, gl * n, tc * p)

    bcomp = jnp.concatenate([state_in(bs_re), state_in(bs_im)], axis=2)
    ccomp = jnp.concatenate([state_out(cp_re[1:]), -state_out(cp_im[1:])], axis=1)
    a4 = jnp.stack([pw_re[tc].reshape(nlb, gl * n), pw_im[tc].reshape(nlb, gl * n)], axis=1)
    return bd.astype(_bf16), bcomp.astype(_bf16), ccomp.astype(_bf16), a4


def _gelu_tanh(x):
    return 0.5 * x * (1.0 + jnp.tanh(math.sqrt(2.0 / math.pi) * (x + 0.044715 * (x * x * x))))


def _s5_expand(bd_ref, bcomp_ref, ccomp_ref, m4_ref, b4_ref, c4_ref):
    tc, p, n = S_TC, S_GROUP_CH, S_STATE
    wide = tc * LANES
    m4_ref[...] = jnp.zeros(m4_ref.shape, m4_ref.dtype)
    for s in range(tc):
        for t in range(s, tc):
            m4_ref[s * LANES:(s + 1) * LANES, t * LANES:(t + 1) * LANES] = bd_ref[0, t - s]

    def iota(shape, axis):
        return lax.broadcasted_iota(jnp.int32, shape, axis)

    lg = lambda v: v.bit_length() - 1
    group_mask = S_LB_GROUPS - 1

    def spread(shape, lane_axis):
        wide_i, comp_i = iota(shape, lane_axis), iota(shape, 1 - lane_axis)
        hit = jnp.logical_and(wide_i >> lg(LANES) == comp_i >> lg(p), wide_i & (p - 1) == comp_i & (p - 1))
        return jnp.where(hit, 1.0, 0.0).astype(_bf16)

    b4 = jnp.dot(spread((wide, tc * p), 0), bcomp_ref[0], preferred_element_type=_f32)
    same = (iota(b4.shape, 0) >> lg(p)) & group_mask == (iota(b4.shape, 1) >> lg(n)) & group_mask
    b4_ref[...] = jnp.where(same, b4, 0.0).astype(b4_ref.dtype)
    c4 = jnp.dot(ccomp_ref[0], spread((tc * p, wide), 1), preferred_element_type=_f32)
    same = (iota(c4.shape, 0) >> lg(n)) & group_mask == (iota(c4.shape, 1) >> lg(p)) & group_mask
    c4_ref[...] = jnp.where(same, c4, 0.0).astype(c4_ref.dtype)


def _s5_kernel(x_ref, bd_ref, bcomp_ref, ccomp_ref, a4_ref, o_ref, m4_ref, b4_ref, c4_ref, s_ref, hp_ref,
               *, nchunks):
    @pl.when(pl.program_id(1) == 0)
    def _():
        _s5_expand(bd_ref, bcomp_ref, ccomp_ref, m4_ref, b4_ref, c4_ref)

    u = jnp.concatenate([x_ref[pl.ds(s, nchunks, stride=S_TC), :].astype(_bf16) for s in range(S_TC)], axis=1)
    s_ref[...] = jnp.dot(u, b4_ref[...], preferred_element_type=_f32)
    half = s_ref.shape[1] // 2
    a_re = a4_ref[0, 0:1, :]
    a_im = a4_ref[0, 1:2, :]

    def step(c8, carry):
        h_re, h_im = carry
        base = pl.multiple_of(c8 * SUBLANES, SUBLANES)
        s_blk = s_ref[pl.ds(base, SUBLANES), :]
        rows_re, rows_im = [], []
        for r in range(SUBLANES):
            rows_re.append(h_re)
            rows_im.append(h_im)
            s_re = s_blk[r:r + 1, 0:half]
            s_im = s_blk[r:r + 1, half:2 * half]
            h_re, h_im = a_re * h_re - a_im * h_im + s_re, a_re * h_im + a_im * h_re + s_im
        hp_ref[pl.ds(base, SUBLANES), :] = jnp.concatenate(
            [jnp.concatenate(rows_re, axis=0), jnp.concatenate(rows_im, axis=0)], axis=1)
        return h_re, h_im

    zero = jnp.zeros((1, half), _f32)
    lax.fori_loop(0, nchunks // SUBLANES, step, (zero, zero))
    y = jnp.dot(u, m4_ref[...], preferred_element_type=_f32)
    y += jnp.dot(hp_ref[...].astype(_bf16), c4_ref[...], preferred_element_type=_f32)
    g = _gelu_tanh(y)
    for t in range(S_TC):
        o_ref[pl.ds(t, nchunks, stride=S_TC), :] = g[:, t * LANES:(t + 1) * LANES]


def _glu_kernel(g_ref, w_ref, b_ref, o_ref):
    g = g_ref[...]
    z = jnp.dot(g.astype(_bf16), w_ref[...], preferred_element_type=_f32) + b_ref[...]
    o_ref[...] = (g * jax.nn.sigmoid(z)).astype(o_ref.dtype)


def s5_mixer(proj, bsz, length, lam_re, lam_im, log_step, b_re, b_im, c_re, c_im, d_skip, w_glu, b_glu):
    m = bsz * length
    nch = length // S_TC
    nlb = S_WIDTH // LANES
    wide = S_TC * LANES
    nstate = S_LB_GROUPS * S_STATE
    bd, bcomp, ccomp, a4 = _s5_operators(lam_re, lam_im, log_step, b_re, b_im, c_re, c_im, d_skip)
    per_block = lambda a: pl.BlockSpec((1,) + a.shape[1:], lambda lb, b: (lb,) + (0,) * (a.ndim - 1))
    g = pl.pallas_call(
        functools.partial(_s5_kernel, nchunks=nch),
        grid=(nlb, bsz),
        in_specs=[pl.BlockSpec((length, LANES), lambda lb, b: (b, COL_U // LANES + lb)),
                  per_block(bd), per_block(bcomp), per_block(ccomp), per_block(a4)],
        out_specs=pl.BlockSpec((length, LANES), lambda lb, b: (b, lb)),
        out_shape=jax.ShapeDtypeStruct((m, S_WIDTH), _f32),
        scratch_shapes=[pltpu.VMEM((wide, wide), _bf16), pltpu.VMEM((wide, 2 * nstate), _bf16),
                        pltpu.VMEM((2 * nstate, wide), _bf16),
                        pltpu.VMEM((nch, 2 * nstate), _f32), pltpu.VMEM((nch, 2 * nstate), _f32)],
        compiler_params=_params("arbitrary", "arbitrary"),
        name="s5_scan",
    )(proj, bd, bcomp, ccomp, a4)
    tm = min(MM_TM, m)
    return pl.pallas_call(
        _glu_kernel,
        grid=(m // tm,),
        in_specs=[pl.BlockSpec((tm, S_WIDTH), lambda i: (i, 0)),
                  pl.BlockSpec((S_WIDTH, S_WIDTH), lambda i: (0, 0)),
                  pl.BlockSpec((1, S_WIDTH), lambda i: (0, 0))],
        out_specs=pl.BlockSpec((tm, S_WIDTH), lambda i: (i, 0)),
        out_shape=jax.ShapeDtypeStruct((m, S_WIDTH), _bf16),
        compiler_params=_params("parallel"),
        name="s5_glu",
    )(g, w_glu.astype(_bf16), b_glu.reshape(1, -1).astype(_f32))


def _t5_bucket(rel):
    half = T5_BUCKETS // 2
    max_exact = half // 2
    ret = jnp.where(rel > 0, half, 0)
    n = jnp.abs(rel)
    nf = jnp.maximum(n, 1).astype(_f32)
    large = max_exact + (jnp.log(nf / max_exact) / math.log(T5_MAX_DIST / max_exact)
                         * (half - max_exact)).astype(jnp.int32)
    large = jnp.minimum(large, half - 1)
    return ret + jnp.where(n < max_exact, n, large)


def _dsa_bias_tiles(rel_bias, t):
    assert t >= T5_MAX_DIST
    hp = lax.Precision.HIGHEST
    r = jnp.arange(t, dtype=jnp.int32)[:, None]
    c = jnp.arange(t, dtype=jnp.int32)[None, :]
    rel = jnp.stack([c - r - t, c - r])
    onehot = (_t5_bucket(rel)[..., None] == jnp.arange(T5_BUCKETS)).astype(_f32)
    far = rel_bias.astype(_f32)[T5_BUCKETS // 2 - 1]
    tiles = jnp.einsum("xrcb,bh->hxrc", onehot, rel_bias.astype(_f32) - far[None, :], precision=hp)
    return tiles * LOG2E


def _dsa_prep_kernel(qi_ref, ki_ref, qc_ref, qsu_ref, qsd_ref, kc_ref, ksu_ref, ksd_ref, qo_ref, ko_ref):
    half = I_ROPE // 2
    rep = (I_HEADS * I_DIM) // LANES
    tile = lambda r: jnp.concatenate([r[...]] * rep, axis=1)
    q = _apply_rope(qi_ref[...], tile(qc_ref), tile(qsu_ref), tile(qsd_ref), half)
    for h in range(I_HEADS):
        qo_ref[0, h] = q[:, h * I_DIM:(h + 1) * I_DIM].astype(_bf16)
    ko_ref[...] = _apply_rope(ki_ref[...], kc_ref[...], ksu_ref[...], ksd_ref[...], half).astype(_bf16)


def _float_key(x):
    b = lax.bitcast_convert_type(x, jnp.int32)
    return b ^ ((b >> 31) & 0x7FFFFFFF)


def _dsa_kernel(qi_ref, wi_ref, kit_ref, qc_ref, kct_ref, vc_ref, bias_ref, o_ref,
                keys_ref, s_ref, wb_ref, qb_ref, mx_ref, l_ref, acc_ref, *, tile, k_sel, log2_scale, idx_scale):
    i = pl.program_id(1)

    w = wi_ref[...]
    for h in range(I_HEADS):
        wb_ref[h] = jnp.broadcast_to(w[:, h:h + 1], (tile, LANES))
    qf = qc_ref[...]
    for h in range(C_HEADS):
        qb_ref[h] = qf[:, h * C_HEAD_DIM:(h + 1) * C_HEAD_DIM].astype(_bf16)

    row_chunk = lax.broadcasted_iota(jnp.int32, (tile, tile), 0) // CHUNK
    col_chunk = lax.broadcasted_iota(jnp.int32, (tile, tile), 1) // CHUNK
    admissible_diag = col_chunk <= row_chunk

    def score_block(j, carry):
        kt = kit_ref[0, j]
        acc = jnp.zeros((tile, tile), _f32)
        for h in range(I_HEADS):
            d = jnp.dot(qi_ref[0, h], kt, preferred_element_type=_f32)
            acc += jnp.maximum(d, 0.0) * _lane_tile(wb_ref[h], tile)
        key = _float_key(acc * idx_scale)
        keys_ref[j] = jnp.where(jnp.logical_or(j < i, admissible_diag), key, INT_MIN)
        return carry

    lax.fori_loop(0, i + 1, score_block, 0)

    def count_ge(t):
        tb = jnp.broadcast_to(t, (tile, LANES))

        def blk(j, c):
            kj = keys_ref[j]
            for s in range(tile // LANES):
                c += jnp.where(kj[:, s * LANES:(s + 1) * LANES] >= tb, 1, 0)
            return c

        c = lax.fori_loop(0, i + 1, blk, jnp.zeros((tile, LANES), jnp.int32))
        return jnp.sum(c, axis=1, keepdims=True)

    def search_cond(carry):
        bit, _, _, settled = carry
        return jnp.logical_and(bit < 32, settled == 0)

    def search_step(carry):
        bit, t, cnt_t, _ = carry
        cand = t + jnp.left_shift(jnp.int32(1), 31 - bit)
        cnt = count_ge(cand)
        take = cnt >= k_sel
        t = jnp.where(take, cand, t)
        cnt_t = jnp.where(take, cnt, cnt_t)
        settled = jnp.min(jnp.where(cnt_t == k_sel, 1, 0))
        return bit + 1, t, cnt_t, settled

    all_selected = ((i + 1) * tile <= k_sel).astype(jnp.int32)
    _, thr, _, _ = lax.while_loop(
        search_cond, search_step,
        (jnp.int32(0), jnp.full((tile, 1), INT_MIN, jnp.int32), jnp.full((tile, 1), -1, jnp.int32), all_selected))
    thr = jnp.maximum(thr, INT_MIN + 1)
    thr_b = jnp.broadcast_to(thr, (tile, tile))

    def mask_block(j, carry):
        mb = jnp.where(keys_ref[j] >= thr_b, 0.0, NEG_BIG).astype(_f32)
        keys_ref[j] = lax.bitcast_convert_type(mb, jnp.int32)
        return carry

    lax.fori_loop(0, i + 1, mask_block, 0)

    grp = DSA_HEAD_GROUP
    rows = grp * tile

    def group_body(g, carry):
        h0 = pl.multiple_of(g * grp, grp)
        q = qb_ref[pl.ds(h0, grp)].reshape(rows, C_HEAD_DIM)
        mx_ref[...] = jnp.full(mx_ref.shape, NEG_BIG, _f32)

        def scores(j):
            s = jnp.dot(q, kct_ref[0, j], preferred_element_type=_f32) * log2_scale
            return s.reshape(grp, tile, tile) + lax.bitcast_convert_type(keys_ref[j], _f32)[None]

        def keep(js, ss):
            mx = mx_ref[...]
            for j, s in zip(js, ss):
                s_ref[:, j] = s
                mx = jnp.maximum(mx, _fold_lanes(s.reshape(rows, tile), jnp.maximum))
            mx_ref[...] = mx

        _for_blocks(jnp.maximum(i - 1, 0), lambda js: keep(js, [scores(j) for j in js]))

        @pl.when(i >= 1)
        def _():
            keep([i - 1], [scores(i - 1) + bias_ref[pl.ds(h0, grp), 0]])

        keep([i], [scores(i) + bias_ref[pl.ds(h0, grp), 1]])

        mx_ref[...] = jnp.broadcast_to(jnp.max(mx_ref[...], axis=1, keepdims=True), mx_ref.shape)
        l_ref[...] = jnp.zeros(l_ref.shape, _f32)
        acc_ref[pl.ds(h0, grp)] = jnp.zeros((grp, tile, C_HEAD_DIM), _f32)

        def sweep2(js):
            m = _lane_tile(mx_ref[...], tile)
            l, acc = l_ref[...], acc_ref[pl.ds(h0, grp)]
            for j in js:
                p = jnp.exp2(s_ref[:, j].reshape(rows, tile) - m)
                l += _fold_lanes(p, jnp.add)
                pv = jnp.dot(p.astype(_bf16), vc_ref[0, j], preferred_element_type=_f32)
                acc += pv.reshape(grp, tile, C_HEAD_DIM)
            l_ref[...] = l
            acc_ref[pl.ds(h0, grp)] = acc

        _for_blocks(i + 1, sweep2)
        denom = jnp.sum(l_ref[...], axis=1, keepdims=True).reshape(grp, tile, 1)
        acc_ref[pl.ds(h0, grp)] = acc_ref[pl.ds(h0, grp)] / denom
        return carry

    lax.fori_loop(0, C_HEADS // grp, group_body, 0)
    for h in range(C_HEADS):
        o_ref[:, h * C_HEAD_DIM:(h + 1) * C_HEAD_DIM] = acc_ref[h].astype(o_ref.dtype)


def dsa_mixer(proj, bsz, length, rel_bias):
    m = bsz * length
    t = min(DSA_T, length)
    nb = length // t
    k_sel = min(TOPK_MAX, length // 4)
    tm = min(PROJ_TM, length)
    nt = length // tm
    iw = I_HEADS * I_DIM
    qtab = _rope_tables(length, I_ROPE, 0, I_DIM)
    qtab = [jnp.concatenate([a.at[:, I_ROPE:].set(1.0) if n == 0 else a] * (LANES // I_DIM), axis=1)
            for n, a in enumerate(qtab)]
    ktab = _rope_tables(length, I_ROPE, 0, LANES)
    ktab = [ktab[0].at[:, I_ROPE:I_DIM].set(1.0), ktab[1], ktab[2]]
    tab = pl.BlockSpec((tm, LANES), lambda i: (i % nt, 0))
    qi, ki = pl.pallas_call(
        _dsa_prep_kernel,
        grid=(m // tm,),
        in_specs=[pl.BlockSpec((tm, iw), lambda i: (i, COL_QI // iw)),
                  pl.BlockSpec((tm, LANES), lambda i: (i, COL_KI // LANES)),
                  tab, tab, tab, tab, tab, tab],
        out_specs=(pl.BlockSpec((1, I_HEADS, tm, I_DIM), lambda i: (i // nt, 0, i % nt, 0)),
                   pl.BlockSpec((tm, LANES), lambda i: (i, 0))),
        out_shape=(jax.ShapeDtypeStruct((bsz, I_HEADS, length, I_DIM), _bf16),
                   jax.ShapeDtypeStruct((m, LANES), _bf16)),
        compiler_params=_params("parallel"),
        name="dsa_prep",
    )(proj, proj, *qtab, *ktab)

    kit = jnp.transpose(ki[:, :I_DIM].reshape(bsz, nb, t, I_DIM), (0, 1, 3, 2))
    kct = jnp.transpose(proj[:, COL_KC:COL_KC + C_HEAD_DIM].astype(_bf16).reshape(bsz, nb, t, C_HEAD_DIM),
                        (0, 1, 3, 2))
    vc = proj[:, COL_VC:COL_VC + C_HEAD_DIM].astype(_bf16).reshape(bsz, nb, t, C_HEAD_DIM)
    bias = _dsa_bias_tiles(rel_bias, t)
    qw = C_HEADS * C_HEAD_DIM
    whole = lambda shape: pl.BlockSpec((1,) + shape, lambda b, i: (b, 0, 0, 0))
    return pl.pallas_call(
        functools.partial(_dsa_kernel, tile=t, k_sel=k_sel, log2_scale=float(C_HEAD_DIM) ** -0.5 * LOG2E,
                          idx_scale=float(I_DIM * I_HEADS) ** -0.5),
        grid=(bsz, nb),
        in_specs=[pl.BlockSpec((1, I_HEADS, t, I_DIM), lambda b, i: (b, 0, i, 0)),
                  pl.BlockSpec((t, LANES), lambda b, i: (b * nb + i, COL_WI // LANES)),
                  whole((nb, I_DIM, t)),
                  pl.BlockSpec((t, qw), lambda b, i: (b * nb + i, COL_QC // qw)),
                  whole((nb, C_HEAD_DIM, t)),
                  whole((nb, t, C_HEAD_DIM)),
                  pl.BlockSpec((C_HEADS, 2, t, t), lambda b, i: (0, 0, 0, 0),
                               pipeline_mode=pl.Buffered(1))],
        out_specs=pl.BlockSpec((t, qw), lambda b, i: (b * nb + i, 0)),
        out_shape=jax.ShapeDtypeStruct((m, qw), _bf16),
        scratch_shapes=[pltpu.VMEM((nb, t, t), jnp.int32),
                        pltpu.VMEM((DSA_HEAD_GROUP, nb, t, t), _f32),
                        pltpu.VMEM((I_HEADS, t, LANES), _f32),
                        pltpu.VMEM((C_HEADS, t, C_HEAD_DIM), _bf16),
                        pltpu.VMEM((DSA_HEAD_GROUP * t, LANES), _f32),
                        pltpu.VMEM((DSA_HEAD_GROUP * t, LANES), _f32),
                        pltpu.VMEM((C_HEADS, t, C_HEAD_DIM), _f32)],
        compiler_params=_params("parallel", "arbitrary"),
        name="dsa_attention",
    )(qi, proj, kit, proj, kct, vc, bias)


def _reorder_w_in(w):
    sizes = (A_Q_LORA, A_KV_LORA, A_ROPE, S_WIDTH, C_HEADS * C_HEAD_DIM, C_HEAD_DIM, C_HEAD_DIM,
             I_HEADS * I_DIM, I_DIM, I_HEADS)
    offs = [0]
    for s in sizes:
        offs.append(offs[-1] + s)
    q_lat, kv_lat, k_rope, u, q_c, k_c, v_c, q_i, k_i, w_i = [w[:, offs[n]:offs[n + 1]] for n in range(10)]
    pad = lambda a, width: jnp.pad(a, ((0, 0), (0, width - a.shape[1])))
    cols = [q_c, q_lat, k_c, v_c, pad(k_rope, LANES), pad(k_i, LANES), pad(w_i, LANES),
            jnp.zeros((w.shape[0], LANES), w.dtype), u, q_i, kv_lat]
    out = jnp.concatenate(cols, axis=1).astype(_bf16)
    assert out.shape[1] == PROJ_WIDTH
    return out


def kernel(x, ln_in_g, ln_in_b, rel_bias, w_in, a_gq, a_wuq, a_gkv, a_wukv, s_lam_re, s_lam_im, s_log_step, s_b_re, s_b_im, s_c_re, s_c_im, s_d, s_w_glu, s_b_glu, w_out, ln1_g, ln1_b, w_ff1, w_ff2, ln2_g, ln2_b):
    bsz, length, d_model = x.shape
    depth = w_in.shape[0]
    alpha = (2 * depth) ** 0.25
    m = bsz * length

    h, hb = layer_norm(x.reshape(m, d_model), ln_in_g, ln_in_b)
    for l in range(depth):
        proj = matmul(hb, _reorder_w_in(w_in[l]), _f32)
        a_out = mla_mixer(proj, bsz, length, a_gq[l], a_wuq[l], a_gkv[l], a_wukv[l])
        b_out = s5_mixer(proj, bsz, length, s_lam_re[l], s_lam_im[l], s_log_step[l], s_b_re[l], s_b_im[l],
                         s_c_re[l], s_c_im[l], s_d[l], s_w_glu[l], s_b_glu[l])
        c_out = dsa_mixer(proj, bsz, length, rel_bias)
        mix = matmul_concat3(a_out, b_out, c_out, w_out[l].astype(_bf16))
        h, hb = layer_norm(mix, ln1_g[l], ln1_b[l], resid=h, alpha=alpha)
        act = matmul(hb, w_ff1[l].astype(_bf16), _bf16, act="relu2")
        ff = matmul_ksplit(act, w_ff2[l].astype(_bf16))
        h, hb = layer_norm(ff, ln2_g[l], ln2_b[l], resid=h, alpha=alpha)
    return h.reshape(bsz, length, d_model)
```

```python
import functools
import math

import jax
import jax.numpy as jnp
from jax import lax
from jax.experimental import pallas as pl
from jax.experimental.pallas import tpu as pltpu

CHUNK = 64
A_HEADS, A_NOPE, A_ROPE, A_VDIM = 12, 128, 64, 128
A_Q_LORA, A_KV_LORA = 768, 512
S_GROUP_CH, S_WIDTH, S_STATE = 16, 1024, 64
S_GROUPS = S_WIDTH // S_GROUP_CH
C_HEADS, C_HEAD_DIM = 12, 128
I_HEADS, I_DIM, I_ROPE = 16, 64, 32
TOPK_MAX = 256
T5_BUCKETS, T5_MAX_DIST = 32, 128
ROPE_THETA = 10000.0
LN_EPS = 1e-5
RMS_EPS = 1e-6

LANES = 128
SUBLANES = 8
VMEM_LIMIT_BYTES = 56 * 1024 * 1024

MM_TM, MM_TN, MM_TK = 1024, 1024, 4096
ROW_TILE = 256
PROJ_TM = 512
ATT_T = 512
DSA_T = 256
DSA_HEAD_GROUP = 4
S_TC = 16
S_LB_GROUPS = LANES // S_GROUP_CH

COL_QC, COL_QLAT, COL_KC, COL_VC, COL_KROPE, COL_KI, COL_WI = 0, 1536, 2304, 2432, 2560, 2688, 2816
COL_U, COL_QI, COL_KVLAT, PROJ_WIDTH = 3072, 4096, 5120, 5632

NEG_BIG = -1e30
LOG2E = math.log2(math.e)
INT_MIN = -2 ** 31

_bf16 = jnp.bfloat16
_f32 = jnp.float32


def _tile(n, pref):
    t = min(pref, n)
    while n % t:
        t //= 2
    return t


def _params(*sem):
    return pltpu.CompilerParams(dimension_semantics=sem, vmem_limit_bytes=VMEM_LIMIT_BYTES)


def _mm_kernel(x_ref, w_ref, o_ref, *, act):
    y = jnp.dot(x_ref[...], w_ref[...], preferred_element_type=_f32)
    if act == "relu2":
        y = jnp.square(jnp.maximum(y, 0.0))
    o_ref[...] = y.astype(o_ref.dtype)


def matmul(x, w, out_dtype, act=None):
    m, k = x.shape
    n = w.shape[1]
    tm, tn = _tile(m, MM_TM), _tile(n, MM_TN)
    return pl.pallas_call(
        functools.partial(_mm_kernel, act=act),
        grid=(m // tm, n // tn),
        in_specs=[pl.BlockSpec((tm, k), lambda i, j: (i, 0)),
                  pl.BlockSpec((k, tn), lambda i, j: (0, j))],
        out_specs=pl.BlockSpec((tm, tn), lambda i, j: (i, j)),
        out_shape=jax.ShapeDtypeStruct((m, n), out_dtype),
        compiler_params=_params("parallel", "parallel"),
        name="matmul",
    )(x, w)


def _mm_acc_kernel(x_ref, w_ref, o_ref):
    y = jnp.dot(x_ref[...], w_ref[...], preferred_element_type=_f32)

    @pl.when(pl.program_id(2) == 0)
    def _():
        o_ref[...] = y

    @pl.when(pl.program_id(2) != 0)
    def _():
        o_ref[...] += y


def matmul_ksplit(x, w):
    m, k = x.shape
    n = w.shape[1]
    tm, tn, tk = min(MM_TM, m), min(MM_TN, n), min(MM_TK, k)
    return pl.pallas_call(
        _mm_acc_kernel,
        grid=(m // tm, n // tn, k // tk),
        in_specs=[pl.BlockSpec((tm, tk), lambda i, j, l: (i, l)),
                  pl.BlockSpec((tk, tn), lambda i, j, l: (l, j))],
        out_specs=pl.BlockSpec((tm, tn), lambda i, j, l: (i, j)),
        out_shape=jax.ShapeDtypeStruct((m, n), _f32),
        compiler_params=_params("parallel", "parallel", "arbitrary"),
        name="matmul_ksplit",
    )(x, w)


def _mm3_kernel(a_ref, b_ref, c_ref, w_ref, o_ref):
    ka, kb = a_ref.shape[1], b_ref.shape[1]
    y = jnp.dot(a_ref[...], w_ref[0:ka, :], preferred_element_type=_f32)
    y += jnp.dot(b_ref[...], w_ref[ka:ka + kb, :], preferred_element_type=_f32)
    y += jnp.dot(c_ref[...], w_ref[ka + kb:, :], preferred_element_type=_f32)
    o_ref[...] = y


def matmul_concat3(a, b, c, w):
    m = a.shape[0]
    k, n = w.shape
    assert a.shape[1] + b.shape[1] + c.shape[1] == k
    tm, tn = _tile(m, MM_TM), _tile(n, MM_TN)
    row = lambda i, j: (i, 0)
    return pl.pallas_call(
        _mm3_kernel,
        grid=(m // tm, n // tn),
        in_specs=[pl.BlockSpec((tm, a.shape[1]), row), pl.BlockSpec((tm, b.shape[1]), row),
                  pl.BlockSpec((tm, c.shape[1]), row),
                  pl.BlockSpec((k, tn), lambda i, j: (0, j))],
        out_specs=pl.BlockSpec((tm, tn), lambda i, j: (i, j)),
        out_shape=jax.ShapeDtypeStruct((m, n), _f32),
        compiler_params=_params("parallel", "parallel"),
        name="matmul_concat3",
    )(a, b, c, w)


def _ln_rows(z, g, b):
    mu = jnp.mean(z, axis=-1, keepdims=True)
    zc = z - mu
    var = jnp.mean(zc * zc, axis=-1, keepdims=True)
    return zc * lax.rsqrt(var + LN_EPS) * g + b


def _ln_kernel(x_ref, g_ref, b_ref, o_ref, ob_ref):
    y = _ln_rows(x_ref[...], g_ref[...], b_ref[...])
    o_ref[...] = y
    ob_ref[...] = y.astype(_bf16)


def _res_ln_kernel(h_ref, y_ref, g_ref, b_ref, o_ref, ob_ref, *, alpha):
    y = _ln_rows(alpha * h_ref[...] + y_ref[...], g_ref[...], b_ref[...])
    o_ref[...] = y
    ob_ref[...] = y.astype(_bf16)


def layer_norm(x, g, b, resid=None, alpha=1.0):
    m, d = x.shape
    tr = min(ROW_TILE, m)
    rows = pl.BlockSpec((tr, d), lambda i: (i, 0))
    vec = pl.BlockSpec((1, d), lambda i: (0, 0))
    out_shape = (jax.ShapeDtypeStruct((m, d), _f32), jax.ShapeDtypeStruct((m, d), _bf16))
    g2, b2 = g.reshape(1, d), b.reshape(1, d)
    if resid is None:
        return pl.pallas_call(_ln_kernel, grid=(m // tr,), in_specs=[rows, vec, vec],
                              out_specs=(rows, rows), out_shape=out_shape,
                              compiler_params=_params("parallel"), name="layer_norm")(x, g2, b2)
    return pl.pallas_call(functools.partial(_res_ln_kernel, alpha=alpha), grid=(m // tr,),
                          in_specs=[rows, rows, vec, vec], out_specs=(rows, rows),
                          out_shape=out_shape, compiler_params=_params("parallel"),
                          name="residual_layer_norm")(resid, x, g2, b2)


def _rope_cos_sin(length, dim):
    inv = ROPE_THETA ** (-jnp.arange(0, dim, 2, dtype=_f32) / dim)
    ang = jnp.arange(length, dtype=_f32)[:, None] * inv[None, :]
    return jnp.cos(ang), jnp.sin(ang)


def _rope_tables(length, dim, lead, width):
    cos, sin = _rope_cos_sin(length, dim)
    half = dim // 2
    zeros = lambda n: jnp.zeros((length, n), _f32)
    tail = width - lead - dim
    c = jnp.concatenate([jnp.ones((length, lead), _f32), cos, cos, zeros(tail)], axis=1)
    s_up = jnp.concatenate([zeros(lead + half), sin, zeros(tail)], axis=1)
    s_dn = jnp.concatenate([zeros(lead), -sin, zeros(half + tail)], axis=1)
    return c, s_up, s_dn


def _apply_rope(x, c, s_up, s_dn, half):
    width = x.shape[-1]
    return x * c + pltpu.roll(x, half, 1) * s_up + pltpu.roll(x, width - half, 1) * s_dn


def _rms_rows(x, g):
    return x * lax.rsqrt(jnp.mean(x * x, axis=-1, keepdims=True) + RMS_EPS) * g


A_QW = 256


def _mla_q_kernel(x_ref, g_ref, w_ref, c_ref, su_ref, sd_ref, o_ref):
    xn = _rms_rows(x_ref[...], g_ref[...]).astype(_bf16)
    y = jnp.dot(xn, w_ref[...], preferred_element_type=_f32)
    c, su, sd = c_ref[...], su_ref[...], sd_ref[...]
    for h in range(A_HEADS):
        yh = _apply_rope(y[:, h * A_QW:(h + 1) * A_QW], c, su, sd, A_ROPE // 2)
        o_ref[0, h] = yh.astype(_bf16)


def _mla_kv_kernel(x_ref, kr_ref, g_ref, w_ref, c_ref, su_ref, sd_ref, k_ref, v_ref):
    xn = _rms_rows(x_ref[...], g_ref[...]).astype(_bf16)
    y = jnp.dot(xn, w_ref[...], preferred_element_type=_f32)
    kr = _apply_rope(kr_ref[...], c_ref[...], su_ref[...], sd_ref[...], A_ROPE // 2).astype(_bf16)
    hw = A_NOPE + A_VDIM
    for h in range(A_HEADS):
        k_ref[0, h] = jnp.concatenate([y[:, h * hw:h * hw + A_NOPE].astype(_bf16), kr], axis=1)
        v_ref[0, h] = y[:, h * hw + A_NOPE:(h + 1) * hw].astype(_bf16)


def _fold_lanes(x, op):
    r = x[:, 0:LANES]
    for n in range(1, x.shape[1] // LANES):
        r = op(r, x[:, n * LANES:(n + 1) * LANES])
    return r


def _lane_tile(x, width):
    return jnp.concatenate([x] * (width // LANES), axis=1)


def _for_blocks(n, fn):
    def quad(g, carry):
        fn([4 * g + u for u in range(4)])
        return carry

    lax.fori_loop(0, n // 4, quad, 0)
    base = (n // 4) * 4

    @pl.when(n % 4 >= 2)
    def _():
        fn([base, base + 1])

    @pl.when(n % 2 == 1)
    def _():
        fn([n - 1])


def _mla_flash_kernel(q_ref, k_ref, v_ref, o_ref, s_ref, mx_ref, l_ref, acc_ref, *, log2_scale, tile):
    i = pl.program_id(2)
    q = q_ref[0, 0]

    def scores(j):
        k = k_ref[0, 0, pl.ds(pl.multiple_of(j * tile, tile), tile), :]
        return lax.dot_general(q, k, (((1,), (1,)), ((), ())), preferred_element_type=_f32) * log2_scale

    def keep(js, ss):
        mx = mx_ref[...]
        for j, s in zip(js, ss):
            s_ref[j] = s
            mx = jnp.maximum(mx, _fold_lanes(s, jnp.maximum))
        mx_ref[...] = mx

    mx_ref[...] = jnp.full(mx_ref.shape, NEG_BIG, _f32)
    _for_blocks(i, lambda js: keep(js, [scores(j) for j in js]))
    row = lax.broadcasted_iota(jnp.int32, (tile, tile), 0) // CHUNK
    col = lax.broadcasted_iota(jnp.int32, (tile, tile), 1) // CHUNK
    keep([i], [jnp.where(col <= row, scores(i), NEG_BIG)])

    mx_ref[...] = jnp.broadcast_to(jnp.max(mx_ref[...], axis=1, keepdims=True), mx_ref.shape)
    l_ref[...] = jnp.zeros(l_ref.shape, _f32)
    acc_ref[...] = jnp.zeros(acc_ref.shape, _f32)

    def sweep2(js):
        m = _lane_tile(mx_ref[...], tile)
        l, acc = l_ref[...], acc_ref[...]
        for j in js:
            p = jnp.exp2(s_ref[j] - m)
            l += _fold_lanes(p, jnp.add)
            v = v_ref[0, 0, pl.ds(pl.multiple_of(j * tile, tile), tile), :]
            acc += jnp.dot(p.astype(_bf16), v, preferred_element_type=_f32)
        l_ref[...] = l
        acc_ref[...] = acc

    _for_blocks(i + 1, sweep2)
    o_ref[...] = (acc_ref[...] / jnp.sum(l_ref[...], axis=1, keepdims=True)).astype(o_ref.dtype)


def mla_mixer(proj, bsz, length, g_q, w_uq, g_kv, w_ukv):
    m = bsz * length
    tm = min(PROJ_TM, length)
    nt = length // tm
    heads = A_HEADS
    qd = A_NOPE + A_ROPE
    wq = jnp.pad(w_uq.reshape(A_Q_LORA, heads, qd), ((0, 0), (0, 0), (0, A_QW - qd)))
    wq = wq.reshape(A_Q_LORA, heads * A_QW).astype(_bf16)
    wkv = w_ukv.astype(_bf16)
    qc, qsu, qsd = _rope_tables(length, A_ROPE, A_NOPE, A_QW)
    kc, ksu, ksd = _rope_tables(length, A_ROPE, 0, LANES)

    tab_q = pl.BlockSpec((tm, A_QW), lambda i: (i % nt, 0))
    tab_k = pl.BlockSpec((tm, LANES), lambda i: (i % nt, 0))
    head_out = lambda w: pl.BlockSpec((1, heads, tm, w), lambda i: (i // nt, 0, i % nt, 0))
    whole = lambda a: pl.BlockSpec(a.shape, lambda i: (0,) * a.ndim)
    gq, gkv = g_q.reshape(1, -1), g_kv.reshape(1, -1)

    q = pl.pallas_call(
        _mla_q_kernel,
        grid=(m // tm,),
        in_specs=[pl.BlockSpec((tm, A_Q_LORA), lambda i: (i, COL_QLAT // A_Q_LORA)),
                  whole(gq), whole(wq), tab_q, tab_q, tab_q],
        out_specs=head_out(A_QW),
        out_shape=jax.ShapeDtypeStruct((bsz, heads, length, A_QW), _bf16),
        compiler_params=_params("parallel"),
        name="mla_q_proj",
    )(proj, gq, wq, qc, qsu, qsd)

    k, v = pl.pallas_call(
        _mla_kv_kernel,
        grid=(m // tm,),
        in_specs=[pl.BlockSpec((tm, A_KV_LORA), lambda i: (i, COL_KVLAT // A_KV_LORA)),
                  pl.BlockSpec((tm, LANES), lambda i: (i, COL_KROPE // LANES)),
                  whole(gkv), whole(wkv), tab_k, tab_k, tab_k],
        out_specs=(head_out(A_QW), head_out(A_VDIM)),
        out_shape=(jax.ShapeDtypeStruct((bsz, heads, length, A_QW), _bf16),
                   jax.ShapeDtypeStruct((bsz, heads, length, A_VDIM), _bf16)),
        compiler_params=_params("parallel"),
        name="mla_kv_proj",
    )(proj, proj, gkv, wkv, kc, ksu, ksd)

    t = min(ATT_T, length)
    nq = length // t
    return pl.pallas_call(
        functools.partial(_mla_flash_kernel, log2_scale=float(qd) ** -0.5 * LOG2E, tile=t),
        grid=(bsz, heads, nq),
        in_specs=[pl.BlockSpec((1, 1, t, A_QW), lambda b, h, i: (b, h, i, 0)),
                  pl.BlockSpec((1, 1, length, A_QW), lambda b, h, i: (b, h, 0, 0)),
                  pl.BlockSpec((1, 1, length, A_VDIM), lambda b, h, i: (b, h, 0, 0))],
        out_specs=pl.BlockSpec((t, A_VDIM), lambda b, h, i: (b * nq + i, h)),
        out_shape=jax.ShapeDtypeStruct((m, heads * A_VDIM), _bf16),
        scratch_shapes=[pltpu.VMEM((nq, t, t), _f32), pltpu.VMEM((t, LANES), _f32),
                        pltpu.VMEM((t, LANES), _f32), pltpu.VMEM((t, A_VDIM), _f32)],
        compiler_params=_params("parallel", "parallel", "arbitrary"),
        name="mla_flash",
    )(q, k, v)


def _s5_operators(lam_re, lam_im, log_step, b_re, b_im, c_re, c_im, d_skip):
    hp = lax.Precision.HIGHEST
    g, n, p, tc = S_GROUPS, S_STATE, S_GROUP_CH, S_TC
    gl = S_LB_GROUPS
    nlb = g // gl
    lr, li = lam_re.astype(_f32), lam_im.astype(_f32)
    dt = jnp.exp(log_step.astype(_f32))[:, None]
    steps = jnp.arange(tc + 1, dtype=_f32)[:, None, None]
    mag = jnp.exp(lr * dt * steps)
    pw_re, pw_im = mag * jnp.cos(li * dt * steps), mag * jnp.sin(li * dt * steps)
    ab_re, ab_im = pw_re[1], pw_im[1]
    den = lr * lr + li * li
    nr, ni = ab_re - 1.0, ab_im
    co_re = (nr * lr + ni * li) / den
    co_im = (ni * lr - nr * li) / den
    br, bi = b_re.astype(_f32), b_im.astype(_f32)
    bb_re = co_re[..., None] * br - co_im[..., None] * bi
    bb_im = co_re[..., None] * bi + co_im[..., None] * br
    cr, ci = c_re.astype(_f32), c_im.astype(_f32)
    cp_re = cr[None] * pw_re[:, :, None, :] - ci[None] * pw_im[:, :, None, :]
    cp_im = cr[None] * pw_im[:, :, None, :] + ci[None] * pw_re[:, :, None, :]
    kern = (jnp.einsum("dgpn,gnq->dgpq", cp_re, bb_re, precision=hp)
            - jnp.einsum("dgpn,gnq->dgpq", cp_im, bb_im, precision=hp))
    pr, pi_ = pw_re[tc - 1 - jnp.arange(tc)], pw_im[tc - 1 - jnp.arange(tc)]
    bs_re = pr[..., None] * bb_re[None] - pi_[..., None] * bb_im[None]
    bs_im = pr[..., None] * bb_im[None] + pi_[..., None] * bb_re[None]
    eye = jnp.eye(gl, dtype=_f32)
    kern = kern.at[0].add(jnp.eye(p, dtype=_f32)[None] * d_skip.astype(_f32)[:, :, None])
    kt = jnp.transpose(kern[:tc].reshape(tc, nlb, gl, p, p), (1, 0, 2, 4, 3))
    bd = (kt[:, :, :, :, None, :] * eye[None, None, :, None, :, None]).reshape(nlb, tc, LANES, LANES)

    def state_in(bs):
        r = jnp.transpose(bs.reshape(tc, nlb, gl, n, p), (1, 0, 4, 2, 3))
        return r.reshape(nlb, tc * p, gl * n)

    def state_out(cpw):
        r = jnp.transpose(cpw.reshape(tc, nlb, gl, p, n), (1, 2, 4, 0, 3))
        return r.reshape(nlb, gl * n, tc * p)

    bcomp = jnp.concatenate([state_in(bs_re), state_in(bs_im)], axis=2)
    ccomp = jnp.concatenate([state_out(cp_re[1:]), -state_out(cp_im[1:])], axis=1)
    a4 = jnp.stack([pw_re[tc].reshape(nlb, gl * n), pw_im[tc].reshape(nlb, gl * n)], axis=1)
    return bd.astype(_bf16), bcomp.astype(_bf16), ccomp.astype(_bf16), a4


def _gelu_tanh(x):
    return 0.5 * x * (1.0 + jnp.tanh(math.sqrt(2.0 / math.pi) * (x + 0.044715 * (x * x * x))))


def _s5_expand(bd_ref, bcomp_ref, ccomp_ref, m4_ref, b4_ref, c4_ref):
    tc, p, n = S_TC, S_GROUP_CH, S_STATE
    wide = tc * LANES
    m4_ref[...] = jnp.zeros(m4_ref.shape, m4_ref.dtype)
    for s in range(tc):
        for t in range(s, tc):
            m4_ref[s * LANES:(s + 1) * LANES, t * LANES:(t + 1) * LANES] = bd_ref[0, t - s]

    def iota(shape, axis):
        return lax.broadcasted_iota(jnp.int32, shape, axis)

    lg = lambda v: v.bit_length() - 1
    group_mask = S_LB_GROUPS - 1

    def spread(shape, lane_axis):
        wide_i, comp_i = iota(shape, lane_axis), iota(shape, 1 - lane_axis)
        hit = jnp.logical_and(wide_i >> lg(LANES) == comp_i >> lg(p), wide_i & (p - 1) == comp_i & (p - 1))
        return jnp.where(hit, 1.0, 0.0).astype(_bf16)

    b4 = jnp.dot(spread((wide, tc * p), 0), bcomp_ref[0], preferred_element_type=_f32)
    same = (iota(b4.shape, 0) >> lg(p)) & group_mask == (iota(b4.shape, 1) >> lg(n)) & group_mask
    b4_ref[...] = jnp.where(same, b4, 0.0).astype(b4_ref.dtype)
    c4 = jnp.dot(ccomp_ref[0], spread((tc * p, wide), 1), preferred_element_type=_f32)
    same = (iota(c4.shape, 0) >> lg(n)) & group_mask == (iota(c4.shape, 1) >> lg(p)) & group_mask
    c4_ref[...] = jnp.where(same, c4, 0.0).astype(c4_ref.dtype)


def _s5_kernel(x_ref, bd_ref, bcomp_ref, ccomp_ref, a4_ref, o_ref, m4_ref, b4_ref, c4_ref, s_ref, hp_ref,
               *, nchunks):
    @pl.when(pl.program_id(1) == 0)
    def _():
        _s5_expand(bd_ref, bcomp_ref, ccomp_ref, m4_ref, b4_ref, c4_ref)

    u = jnp.concatenate([x_ref[pl.ds(s, nchunks, stride=S_TC), :].astype(_bf16) for s in range(S_TC)], axis=1)
    s_ref[...] = jnp.dot(u, b4_ref[...], preferred_element_type=_f32)
    half = s_ref.shape[1] // 2
    a_re = a4_ref[0, 0:1, :]
    a_im = a4_ref[0, 1:2, :]

    def step(c8, carry):
        h_re, h_im = carry
        base = pl.multiple_of(c8 * SUBLANES, SUBLANES)
        s_blk = s_ref[pl.ds(base, SUBLANES), :]
        rows_re, rows_im = [], []
        for r in range(SUBLANES):
            rows_re.append(h_re)
            rows_im.append(h_im)
            s_re = s_blk[r:r + 1, 0:half]
            s_im = s_blk[r:r + 1, half:2 * half]
            h_re, h_im = a_re * h_re - a_im * h_im + s_re, a_re * h_im + a_im * h_re + s_im
        hp_ref[pl.ds(base, SUBLANES), :] = jnp.concatenate(
            [jnp.concatenate(rows_re, axis=0), jnp.concatenate(rows_im, axis=0)], axis=1)
        return h_re, h_im

    zero = jnp.zeros((1, half), _f32)
    lax.fori_loop(0, nchunks // SUBLANES, step, (zero, zero))
    y = jnp.dot(u, m4_ref[...], preferred_element_type=_f32)
    y += jnp.dot(hp_ref[...].astype(_bf16), c4_ref[...], preferred_element_type=_f32)
    g = _gelu_tanh(y)
    for t in range(S_TC):
        o_ref[pl.ds(t, nchunks, stride=S_TC), :] = g[:, t * LANES:(t + 1) * LANES]


def _glu_kernel(g_ref, w_ref, b_ref, o_ref):
    g = g_ref[...]
    z = jnp.dot(g.astype(_bf16), w_ref[...], preferred_element_type=_f32) + b_ref[...]
    o_ref[...] = (g * jax.nn.sigmoid(z)).astype(o_ref.dtype)


def s5_mixer(proj, bsz, length, lam_re, lam_im, log_step, b_re, b_im, c_re, c_im, d_skip, w_glu, b_glu):
    m = bsz * length
    nch = length // S_TC
    nlb = S_WIDTH // LANES
    wide = S_TC * LANES
    nstate = S_LB_GROUPS * S_STATE
    bd, bcomp, ccomp, a4 = _s5_operators(lam_re, lam_im, log_step, b_re, b_im, c_re, c_im, d_skip)
    per_block = lambda a: pl.BlockSpec((1,) + a.shape[1:], lambda lb, b: (lb,) + (0,) * (a.ndim - 1))
    g = pl.pallas_call(
        functools.partial(_s5_kernel, nchunks=nch),
        grid=(nlb, bsz),
        in_specs=[pl.BlockSpec((length, LANES), lambda lb, b: (b, COL_U // LANES + lb)),
                  per_block(bd), per_block(bcomp), per_block(ccomp), per_block(a4)],
        out_specs=pl.BlockSpec((length, LANES), lambda lb, b: (b, lb)),
        out_shape=jax.ShapeDtypeStruct((m, S_WIDTH), _f32),
        scratch_shapes=[pltpu.VMEM((wide, wide), _bf16), pltpu.VMEM((wide, 2 * nstate), _bf16),
                        pltpu.VMEM((2 * nstate, wide), _bf16),
                        pltpu.VMEM((nch, 2 * nstate), _f32), pltpu.VMEM((nch, 2 * nstate), _f32)],
        compiler_params=_params("arbitrary", "arbitrary"),
        name="s5_scan",
    )(proj, bd, bcomp, ccomp, a4)
    tm = min(MM_TM, m)
    return pl.pallas_call(
        _glu_kernel,
        grid=(m // tm,),
        in_specs=[pl.BlockSpec((tm, S_WIDTH), lambda i: (i, 0)),
                  pl.BlockSpec((S_WIDTH, S_WIDTH), lambda i: (0, 0)),
                  pl.BlockSpec((1, S_WIDTH), lambda i: (0, 0))],
        out_specs=pl.BlockSpec((tm, S_WIDTH), lambda i: (i, 0)),
        out_shape=jax.ShapeDtypeStruct((m, S_WIDTH), _bf16),
        compiler_params=_params("parallel"),
        name="s5_glu",
    )(g, w_glu.astype(_bf16), b_glu.reshape(1, -1).astype(_f32))


def _t5_bucket(rel):
    half = T5_BUCKETS // 2
    max_exact = half // 2
    ret = jnp.where(rel > 0, half, 0)
    n = jnp.abs(rel)
    nf = jnp.maximum(n, 1).astype(_f32)
    large = max_exact + (jnp.log(nf / max_exact) / math.log(T5_MAX_DIST / max_exact)
                         * (half - max_exact)).astype(jnp.int32)
    large = jnp.minimum(large, half - 1)
    return ret + jnp.where(n < max_exact, n, large)


def _dsa_bias_tiles(rel_bias, t):
    assert t >= T5_MAX_DIST
    hp = lax.Precision.HIGHEST
    r = jnp.arange(t, dtype=jnp.int32)[:, None]
    c = jnp.arange(t, dtype=jnp.int32)[None, :]
    rel = jnp.stack([c - r - t, c - r])
    onehot = (_t5_bucket(rel)[..., None] == jnp.arange(T5_BUCKETS)).astype(_f32)
    far = rel_bias.astype(_f32)[T5_BUCKETS // 2 - 1]
    tiles = jnp.einsum("xrcb,bh->hxrc", onehot, rel_bias.astype(_f32) - far[None, :], precision=hp)
    return tiles * LOG2E


def _dsa_prep_kernel(qi_ref, ki_ref, qc_ref, qsu_ref, qsd_ref, kc_ref, ksu_ref, ksd_ref, qo_ref, ko_ref):
    half = I_ROPE // 2
    rep = (I_HEADS * I_DIM) // LANES
    tile = lambda r: jnp.concatenate([r[...]] * rep, axis=1)
    q = _apply_rope(qi_ref[...], tile(qc_ref), tile(qsu_ref), tile(qsd_ref), half)
    for h in range(I_HEADS):
        qo_ref[0, h] = q[:, h * I_DIM:(h + 1) * I_DIM].astype(_bf16)
    ko_ref[...] = _apply_rope(ki_ref[...], kc_ref[...], ksu_ref[...], ksd_ref[...], half).astype(_bf16)


def _float_key(x):
    b = lax.bitcast_convert_type(x, jnp.int32)
    return b ^ ((b >> 31) & 0x7FFFFFFF)


def _dsa_kernel(qi_ref, wi_ref, kit_ref, qc_ref, kct_ref, vc_ref, bias_ref, o_ref,
                keys_ref, hi_ref, lo_ref, s_ref, wb_ref, qb_ref, mx_ref, l_ref, acc_ref,
                *, tile, k_sel, log2_scale, idx_scale):
    i = pl.program_id(1)

    w = wi_ref[...]
    for h in range(I_HEADS):
        wb_ref[h] = jnp.broadcast_to(w[:, h:h + 1], (tile, LANES))
    qf = qc_ref[...]
    for h in range(C_HEADS):
        qb_ref[h] = qf[:, h * C_HEAD_DIM:(h + 1) * C_HEAD_DIM].astype(_bf16)

    row_chunk = lax.broadcasted_iota(jnp.int32, (tile, tile), 0) // CHUNK
    col_chunk = lax.broadcasted_iota(jnp.int32, (tile, tile), 1) // CHUNK
    admissible_diag = col_chunk <= row_chunk

    def score_block(j, carry):
        kt = kit_ref[0, j]
        acc = jnp.zeros((tile, tile), _f32)
        for h in range(I_HEADS):
            d = jnp.dot(qi_ref[0, h], kt, preferred_element_type=_f32)
            acc += jnp.maximum(d, 0.0) * _lane_tile(wb_ref[h], tile)
        key = _float_key(acc * idx_scale)
        keys_ref[j] = jnp.where(jnp.logical_or(j < i, admissible_diag), key, INT_MIN)
        return carry

    lax.fori_loop(0, i + 1, score_block, 0)

    half_min = -(1 << 15)

    def split_block(j, carry):
        k32 = keys_ref[j]
        hi_ref[j] = (k32 >> 16).astype(jnp.int16)
        lo_ref[j] = ((k32 & 0xFFFF) + half_min).astype(jnp.int16)
        return carry

    lax.fori_loop(0, i + 1, split_block, 0)

    def count16(ref, t, strict=False):
        tb = jnp.broadcast_to(t, (tile, LANES)).astype(jnp.int16)
        one, zero = jnp.int16(1), jnp.int16(0)

        def blk(j, c):
            kj = ref[j]
            for s in range(tile // LANES):
                ks = kj[:, s * LANES:(s + 1) * LANES]
                c = c + jnp.where(ks > tb if strict else ks >= tb, one, zero)
            return c

        c = lax.fori_loop(0, i + 1, blk, jnp.zeros((tile, LANES), jnp.int16))
        return jnp.sum(c.astype(jnp.int32), axis=1, keepdims=True)

    def search16(ref, base_cnt, settled):
        def cond(carry):
            bit, _, _, done = carry
            return jnp.logical_and(bit < 16, done == 0)

        def step(carry):
            bit, t, cnt_t, _ = carry
            cand = t + jnp.left_shift(jnp.int32(1), 15 - bit)
            cnt = base_cnt + count16(ref, cand)
            take = cnt >= k_sel
            t = jnp.where(take, cand, t)
            cnt_t = jnp.where(take, cnt, cnt_t)
            return bit + 1, t, cnt_t, jnp.min(jnp.where(cnt_t == k_sel, 1, 0))

        init = (jnp.int32(0), jnp.full((tile, 1), half_min, jnp.int32), jnp.full((tile, 1), -1, jnp.int32), settled)
        _, t, _, done = lax.while_loop(cond, step, init)
        return t, done

    all_selected = ((i + 1) * tile <= k_sel).astype(jnp.int32)
    hi_thr, settled = search16(hi_ref, jnp.zeros((tile, 1), jnp.int32), all_selected)

    def low_half():
        above = count16(hi_ref, hi_thr, strict=True)
        hb = jnp.broadcast_to(hi_thr, (tile, tile)).astype(jnp.int16)

        def restrict(j, carry):
            lo_ref[j] = jnp.where(hi_ref[j] == hb, lo_ref[j], jnp.int16(half_min))
            return carry

        lax.fori_loop(0, i + 1, restrict, 0)
        lo_thr, _ = search16(lo_ref, above, jnp.int32(0))
        return lo_thr - half_min

    low = lax.cond(settled == 1, lambda: jnp.zeros((tile, 1), jnp.int32), low_half)
    thr = hi_thr * 65536 + low
    thr = jnp.maximum(thr, INT_MIN + 1)
    thr_b = jnp.broadcast_to(thr, (tile, tile))

    def mask_block(j, carry):
        mb = jnp.where(keys_ref[j] >= thr_b, 0.0, NEG_BIG).astype(_f32)
        keys_ref[j] = lax.bitcast_convert_type(mb, jnp.int32)
        return carry

    lax.fori_loop(0, i + 1, mask_block, 0)

    grp = DSA_HEAD_GROUP
    rows = grp * tile

    def group_body(g, carry):
        h0 = pl.multiple_of(g * grp, grp)
        q = qb_ref[pl.ds(h0, grp)].reshape(rows, C_HEAD_DIM)
        mx_ref[...] = jnp.full(mx_ref.shape, NEG_BIG, _f32)

        def scores(j):
            s = jnp.dot(q, kct_ref[0, j], preferred_element_type=_f32) * log2_scale
            return s.reshape(grp, tile, tile) + lax.bitcast_convert_type(keys_ref[j], _f32)[None]

        def keep(js, ss):
            mx = mx_ref[...]
            for j, s in zip(js, ss):
                s_ref[:, j] = s
                mx = jnp.maximum(mx, _fold_lanes(s.reshape(rows, tile), jnp.maximum))
            mx_ref[...] = mx

        _for_blocks(jnp.maximum(i - 1, 0), lambda js: keep(js, [scores(j) for j in js]))

        @pl.when(i >= 1)
        def _():
            keep([i - 1], [scores(i - 1) + bias_ref[pl.ds(h0, grp), 0]])

        keep([i], [scores(i) + bias_ref[pl.ds(h0, grp), 1]])

        mx_ref[...] = jnp.broadcast_to(jnp.max(mx_ref[...], axis=1, keepdims=True), mx_ref.shape)
        l_ref[...] = jnp.zeros(l_ref.shape, _f32)
        acc_ref[pl.ds(h0, grp)] = jnp.zeros((grp, tile, C_HEAD_DIM), _f32)

        def sweep2(js):
            m = _lane_tile(mx_ref[...], tile)
            l, acc = l_ref[...], acc_ref[pl.ds(h0, grp)]
            for j in js:
                p = jnp.exp2(s_ref[:, j].reshape(rows, tile) - m)
                l += _fold_lanes(p, jnp.add)
                pv = jnp.dot(p.astype(_bf16), vc_ref[0, j], preferred_element_type=_f32)
                acc += pv.reshape(grp, tile, C_HEAD_DIM)
            l_ref[...] = l
            acc_ref[pl.ds(h0, grp)] = acc

        _for_blocks(i + 1, sweep2)
        denom = jnp.sum(l_ref[...], axis=1, keepdims=True).reshape(grp, tile, 1)
        acc_ref[pl.ds(h0, grp)] = acc_ref[pl.ds(h0, grp)] / denom
        return carry

    lax.fori_loop(0, C_HEADS // grp, group_body, 0)
    for h in range(C_HEADS):
        o_ref[:, h * C_HEAD_DIM:(h + 1) * C_HEAD_DIM] = acc_ref[h].astype(o_ref.dtype)


def dsa_mixer(proj, bsz, length, rel_bias):
    m = bsz * length
    t = min(DSA_T, length)
    nb = length // t
    k_sel = min(TOPK_MAX, length // 4)
    tm = min(PROJ_TM, length)
    nt = length // tm
    iw = I_HEADS * I_DIM
    qtab = _rope_tables(length, I_ROPE, 0, I_DIM)
    qtab = [jnp.concatenate([a.at[:, I_ROPE:].set(1.0) if n == 0 else a] * (LANES // I_DIM), axis=1)
            for n, a in enumerate(qtab)]
    ktab = _rope_tables(length, I_ROPE, 0, LANES)
    ktab = [ktab[0].at[:, I_ROPE:I_DIM].set(1.0), ktab[1], ktab[2]]
    tab = pl.BlockSpec((tm, LANES), lambda i: (i % nt, 0))
    qi, ki = pl.pallas_call(
        _dsa_prep_kernel,
        grid=(m // tm,),
        in_specs=[pl.BlockSpec((tm, iw), lambda i: (i, COL_QI // iw)),
                  pl.BlockSpec((tm, LANES), lambda i: (i, COL_KI // LANES)),
                  tab, tab, tab, tab, tab, tab],
        out_specs=(pl.BlockSpec((1, I_HEADS, tm, I_DIM), lambda i: (i // nt, 0, i % nt, 0)),
                   pl.BlockSpec((tm, LANES), lambda i: (i, 0))),
        out_shape=(jax.ShapeDtypeStruct((bsz, I_HEADS, length, I_DIM), _bf16),
                   jax.ShapeDtypeStruct((m, LANES), _bf16)),
        compiler_params=_params("parallel"),
        name="dsa_prep",
    )(proj, proj, *qtab, *ktab)

    kit = jnp.transpose(ki[:, :I_DIM].reshape(bsz, nb, t, I_DIM), (0, 1, 3, 2))
    kct = jnp.transpose(proj[:, COL_KC:COL_KC + C_HEAD_DIM].astype(_bf16).reshape(bsz, nb, t, C_HEAD_DIM),
                        (0, 1, 3, 2))
    vc = proj[:, COL_VC:COL_VC + C_HEAD_DIM].astype(_bf16).reshape(bsz, nb, t, C_HEAD_DIM)
    bias = _dsa_bias_tiles(rel_bias, t)
    qw = C_HEADS * C_HEAD_DIM
    whole = lambda shape: pl.BlockSpec((1,) + shape, lambda b, i: (b, 0, 0, 0))
    return pl.pallas_call(
        functools.partial(_dsa_kernel, tile=t, k_sel=k_sel, log2_scale=float(C_HEAD_DIM) ** -0.5 * LOG2E,
                          idx_scale=float(I_DIM * I_HEADS) ** -0.5),
        grid=(bsz, nb),
        in_specs=[pl.BlockSpec((1, I_HEADS, t, I_DIM), lambda b, i: (b, 0, i, 0)),
                  pl.BlockSpec((t, LANES), lambda b, i: (b * nb + i, COL_WI // LANES)),
                  whole((nb, I_DIM, t)),
                  pl.BlockSpec((t, qw), lambda b, i: (b * nb + i, COL_QC // qw)),
                  whole((nb, C_HEAD_DIM, t)),
                  whole((nb, t, C_HEAD_DIM)),
                  pl.BlockSpec((C_HEADS, 2, t, t), lambda b, i: (0, 0, 0, 0),
                               pipeline_mode=pl.Buffered(1))],
        out_specs=pl.BlockSpec((t, qw), lambda b, i: (b * nb + i, 0)),
        out_shape=jax.ShapeDtypeStruct((m, qw), _bf16),
        scratch_shapes=[pltpu.VMEM((nb, t, t), jnp.int32),
                        pltpu.VMEM((nb, t, t), jnp.int16),
                        pltpu.VMEM((nb, t, t), jnp.int16),
                        pltpu.VMEM((DSA_HEAD_GROUP, nb, t, t), _f32),
                        pltpu.VMEM((I_HEADS, t, LANES), _f32),
                        pltpu.VMEM((C_HEADS, t, C_HEAD_DIM), _bf16),
                        pltpu.VMEM((DSA_HEAD_GROUP * t, LANES), _f32),
                        pltpu.VMEM((DSA_HEAD_GROUP * t, LANES), _f32),
                        pltpu.VMEM((C_HEADS, t, C_HEAD_DIM), _f32)],
        compiler_params=_params("parallel", "arbitrary"),
        name="dsa_attention",
    )(qi, proj, kit, proj, kct, vc, bias)


def _reorder_w_in(w):
    sizes = (A_Q_LORA, A_KV_LORA, A_ROPE, S_WIDTH, C_HEADS * C_HEAD_DIM, C_HEAD_DIM, C_HEAD_DIM,
             I_HEADS * I_DIM, I_DIM, I_HEADS)
    offs = [0]
    for s in sizes:
        offs.append(offs[-1] + s)
    q_lat, kv_lat, k_rope, u, q_c, k_c, v_c, q_i, k_i, w_i = [w[:, offs[n]:offs[n + 1]] for n in range(10)]
    pad = lambda a, width: jnp.pad(a, ((0, 0), (0, width - a.shape[1])))
    cols = [q_c, q_lat, k_c, v_c, pad(k_rope, LANES), pad(k_i, LANES), pad(w_i, LANES),
            jnp.zeros((w.shape[0], LANES), w.dtype), u, q_i, kv_lat]
    out = jnp.concatenate(cols, axis=1).astype(_bf16)
    assert out.shape[1] == PROJ_WIDTH
    return out


def kernel(x, ln_in_g, ln_in_b, rel_bias, w_in, a_gq, a_wuq, a_gkv, a_wukv, s_lam_re, s_lam_im, s_log_step, s_b_re, s_b_im, s_c_re, s_c_im, s_d, s_w_glu, s_b_glu, w_out, ln1_g, ln1_b, w_ff1, w_ff2, ln2_g, ln2_b):
    bsz, length, d_model = x.shape
    depth = w_in.shape[0]
    alpha = (2 * depth) ** 0.25
    m = bsz * length

    h, hb = layer_norm(x.reshape(m, d_model), ln_in_g, ln_in_b)
    for l in range(depth):
        proj = matmul(hb, _reorder_w_in(w_in[l]), _f32)
        a_out = mla_mixer(proj, bsz, length, a_gq[l], a_wuq[l], a_gkv[l], a_wukv[l])
        b_out = s5_mixer(proj, bsz, length, s_lam_re[l], s_lam_im[l], s_log_step[l], s_b_re[l], s_b_im[l],
                         s_c_re[l], s_c_im[l], s_d[l], s_w_glu[l], s_b_glu[l])
        c_out = dsa_mixer(proj, bsz, length, rel_bias)
        mix = matmul_concat3(a_out, b_out, c_out, w_out[l].astype(_bf16))
        h, hb = layer_norm(mix, ln1_g[l], ln1_b[l], resid=h, alpha=alpha)
        act = matmul(hb, w_ff1[l].astype(_bf16), _bf16, act="relu2")
        ff = matmul_ksplit(act, w_ff2[l].astype(_bf16))
        h, hb = layer_norm(ff, ln2_g[l], ln2_b[l], resid=h, alpha=alpha)
    return h.reshape(bsz, length, d_model)
```

```python
import functools
import math

import jax
import jax.numpy as jnp
from jax import lax
from jax.experimental import pallas as pl
from jax.experimental.pallas import tpu as pltpu

CHUNK = 64
A_HEADS, A_NOPE, A_ROPE, A_VDIM = 12, 128, 64, 128
A_Q_LORA, A_KV_LORA = 768, 512
S_GROUP_CH, S_WIDTH, S_STATE = 16, 1024, 64
S_GROUPS = S_WIDTH // S_GROUP_CH
C_HEADS, C_HEAD_DIM = 12, 128
I_HEADS, I_DIM, I_ROPE = 16, 64, 32
TOPK_MAX = 256
T5_BUCKETS, T5_MAX_DIST = 32, 128
ROPE_THETA = 10000.0
LN_EPS = 1e-5
RMS_EPS = 1e-6

LANES = 128
SUBLANES = 8
VMEM_LIMIT_BYTES = 56 * 1024 * 1024

MM_TM, MM_TN, MM_TK = 1024, 1024, 4096
ROW_TILE = 256
PROJ_TM = 512
ATT_T = 512
DSA_T = 256
DSA_HEAD_GROUP = 4
S_TC = 16
S_LB_GROUPS = LANES // S_GROUP_CH

COL_QC, COL_QLAT, COL_KC, COL_VC, COL_KROPE, COL_KI, COL_WI = 0, 1536, 2304, 2432, 2560, 2688, 2816
COL_U, COL_QI, COL_KVLAT, PROJ_WIDTH = 3072, 4096, 5120, 5632

NEG_BIG = -1e30
LOG2E = math.log2(math.e)
INT_MIN = -2 ** 31

_bf16 = jnp.bfloat16
_f32 = jnp.float32


def _tile(n, pref):
    t = min(pref, n)
    while n % t:
        t //= 2
    return t


def _params(*sem):
    return pltpu.CompilerParams(dimension_semantics=sem, vmem_limit_bytes=VMEM_LIMIT_BYTES)


def _mm_kernel(x_ref, w_ref, o_ref, *, act):
    y = jnp.dot(x_ref[...], w_ref[...].astype(_bf16), preferred_element_type=_f32)
    if act == "relu2":
        y = jnp.square(jnp.maximum(y, 0.0))
    o_ref[...] = y.astype(o_ref.dtype)


def _layer_block(w, layer, shape, index):
    if w.ndim == 2:
        return pl.BlockSpec(shape, index)
    return pl.BlockSpec((None,) + shape, lambda *g: (layer,) + index(*g))


def matmul(x, w, out_dtype, act=None, layer=0):
    m, k = x.shape
    n = w.shape[-1]
    tm, tn = _tile(m, MM_TM), _tile(n, MM_TN)
    x_mode = dict(pipeline_mode=pl.Buffered(1)) if w.dtype == _f32 else {}
    return pl.pallas_call(
        functools.partial(_mm_kernel, act=act),
        grid=(m // tm, n // tn),
        in_specs=[pl.BlockSpec((tm, k), lambda i, j: (i, 0), **x_mode),
                  _layer_block(w, layer, (k, tn), lambda i, j: (0, j))],
        out_specs=pl.BlockSpec((tm, tn), lambda i, j: (i, j)),
        out_shape=jax.ShapeDtypeStruct((m, n), out_dtype),
        compiler_params=_params("parallel", "parallel"),
        name="matmul",
    )(x, w)


def _mm_acc_kernel(x_ref, w_ref, o_ref):
    y = jnp.dot(x_ref[...], w_ref[...], preferred_element_type=_f32)

    @pl.when(pl.program_id(2) == 0)
    def _():
        o_ref[...] = y

    @pl.when(pl.program_id(2) != 0)
    def _():
        o_ref[...] += y


def matmul_ksplit(x, w):
    m, k = x.shape
    n = w.shape[1]
    tm, tn, tk = min(MM_TM, m), min(MM_TN, n), min(MM_TK, k)
    return pl.pallas_call(
        _mm_acc_kernel,
        grid=(m // tm, n // tn, k // tk),
        in_specs=[pl.BlockSpec((tm, tk), lambda i, j, l: (i, l)),
                  pl.BlockSpec((tk, tn), lambda i, j, l: (l, j))],
        out_specs=pl.BlockSpec((tm, tn), lambda i, j, l: (i, j)),
        out_shape=jax.ShapeDtypeStruct((m, n), _f32),
        compiler_params=_params("parallel", "parallel", "arbitrary"),
        name="matmul_ksplit",
    )(x, w)


def _mm3_kernel(a_ref, b_ref, c_ref, w_ref, o_ref):
    ka, kb = a_ref.shape[1], b_ref.shape[1]
    y = jnp.dot(a_ref[...], w_ref[0:ka, :], preferred_element_type=_f32)
    y += jnp.dot(b_ref[...], w_ref[ka:ka + kb, :], preferred_element_type=_f32)
    y += jnp.dot(c_ref[...], w_ref[ka + kb:, :], preferred_element_type=_f32)
    o_ref[...] = y


def matmul_concat3(a, b, c, w):
    m = a.shape[0]
    k, n = w.shape
    assert a.shape[1] + b.shape[1] + c.shape[1] == k
    tm, tn = _tile(m, MM_TM), _tile(n, MM_TN)
    row = lambda i, j: (i, 0)
    return pl.pallas_call(
        _mm3_kernel,
        grid=(m // tm, n // tn),
        in_specs=[pl.BlockSpec((tm, a.shape[1]), row), pl.BlockSpec((tm, b.shape[1]), row),
                  pl.BlockSpec((tm, c.shape[1]), row),
                  pl.BlockSpec((k, tn), lambda i, j: (0, j))],
        out_specs=pl.BlockSpec((tm, tn), lambda i, j: (i, j)),
        out_shape=jax.ShapeDtypeStruct((m, n), _f32),
        compiler_params=_params("parallel", "parallel"),
        name="matmul_concat3",
    )(a, b, c, w)


def _ln_rows(z, g, b):
    mu = jnp.mean(z, axis=-1, keepdims=True)
    zc = z - mu
    var = jnp.mean(zc * zc, axis=-1, keepdims=True)
    return zc * lax.rsqrt(var + LN_EPS) * g + b


def _ln_kernel(x_ref, g_ref, b_ref, o_ref, ob_ref):
    y = _ln_rows(x_ref[...], g_ref[...], b_ref[...])
    o_ref[...] = y
    ob_ref[...] = y.astype(_bf16)


def _res_ln_kernel(h_ref, y_ref, g_ref, b_ref, o_ref, ob_ref, *, alpha):
    y = _ln_rows(alpha * h_ref[...] + y_ref[...], g_ref[...], b_ref[...])
    o_ref[...] = y
    ob_ref[...] = y.astype(_bf16)


def layer_norm(x, g, b, resid=None, alpha=1.0):
    m, d = x.shape
    tr = min(ROW_TILE, m)
    rows = pl.BlockSpec((tr, d), lambda i: (i, 0))
    vec = pl.BlockSpec((1, d), lambda i: (0, 0))
    out_shape = (jax.ShapeDtypeStruct((m, d), _f32), jax.ShapeDtypeStruct((m, d), _bf16))
    g2, b2 = g.reshape(1, d), b.reshape(1, d)
    if resid is None:
        return pl.pallas_call(_ln_kernel, grid=(m // tr,), in_specs=[rows, vec, vec],
                              out_specs=(rows, rows), out_shape=out_shape,
                              compiler_params=_params("parallel"), name="layer_norm")(x, g2, b2)
    return pl.pallas_call(functools.partial(_res_ln_kernel, alpha=alpha), grid=(m // tr,),
                          in_specs=[rows, rows, vec, vec], out_specs=(rows, rows),
                          out_shape=out_shape, compiler_params=_params("parallel"),
                          name="residual_layer_norm")(resid, x, g2, b2)


def _rope_cos_sin(length, dim):
    inv = ROPE_THETA ** (-jnp.arange(0, dim, 2, dtype=_f32) / dim)
    ang = jnp.arange(length, dtype=_f32)[:, None] * inv[None, :]
    return jnp.cos(ang), jnp.sin(ang)


def _rope_tables(length, dim, lead, width):
    cos, sin = _rope_cos_sin(length, dim)
    half = dim // 2
    zeros = lambda n: jnp.zeros((length, n), _f32)
    tail = width - lead - dim
    c = jnp.concatenate([jnp.ones((length, lead), _f32), cos, cos, zeros(tail)], axis=1)
    s_up = jnp.concatenate([zeros(lead + half), sin, zeros(tail)], axis=1)
    s_dn = jnp.concatenate([zeros(lead), -sin, zeros(half + tail)], axis=1)
    return c, s_up, s_dn


def _apply_rope(x, c, s_up, s_dn, half):
    width = x.shape[-1]
    return x * c + pltpu.roll(x, half, 1) * s_up + pltpu.roll(x, width - half, 1) * s_dn


def _rms_rows(x, g):
    return x * lax.rsqrt(jnp.mean(x * x, axis=-1, keepdims=True) + RMS_EPS) * g


A_QW = 256


def _mla_q_kernel(x_ref, g_ref, w_ref, c_ref, su_ref, sd_ref, o_ref):
    xn = _rms_rows(x_ref[...], g_ref[...]).astype(_bf16)
    y = jnp.dot(xn, w_ref[...], preferred_element_type=_f32)
    c, su, sd = c_ref[...], su_ref[...], sd_ref[...]
    for h in range(A_HEADS):
        yh = _apply_rope(y[:, h * A_QW:(h + 1) * A_QW], c, su, sd, A_ROPE // 2)
        o_ref[0, h] = yh.astype(_bf16)


def _mla_kv_kernel(x_ref, kr_ref, g_ref, w_ref, c_ref, su_ref, sd_ref, k_ref, v_ref):
    xn = _rms_rows(x_ref[...], g_ref[...]).astype(_bf16)
    y = jnp.dot(xn, w_ref[...], preferred_element_type=_f32)
    kr = _apply_rope(kr_ref[...], c_ref[...], su_ref[...], sd_ref[...], A_ROPE // 2).astype(_bf16)
    hw = A_NOPE + A_VDIM
    for h in range(A_HEADS):
        k_ref[0, h] = jnp.concatenate([y[:, h * hw:h * hw + A_NOPE].astype(_bf16), kr], axis=1)
        v_ref[0, h] = y[:, h * hw + A_NOPE:(h + 1) * hw].astype(_bf16)


def _fold_lanes(x, op):
    r = x[:, 0:LANES]
    for n in range(1, x.shape[1] // LANES):
        r = op(r, x[:, n * LANES:(n + 1) * LANES])
    return r


def _lane_tile(x, width):
    return jnp.concatenate([x] * (width // LANES), axis=1)


def _for_blocks(n, fn):
    def quad(g, carry):
        fn([4 * g + u for u in range(4)])
        return carry

    lax.fori_loop(0, n // 4, quad, 0)
    base = (n // 4) * 4

    @pl.when(n % 4 >= 2)
    def _():
        fn([base, base + 1])

    @pl.when(n % 2 == 1)
    def _():
        fn([n - 1])


def _mla_flash_kernel(q_ref, k_ref, v_ref, o_ref, s_ref, mx_ref, l_ref, acc_ref, *, log2_scale, tile):
    i = pl.program_id(2)
    q = q_ref[0, 0]

    def scores(j):
        k = k_ref[0, 0, pl.ds(pl.multiple_of(j * tile, tile), tile), :]
        return lax.dot_general(q, k, (((1,), (1,)), ((), ())), preferred_element_type=_f32) * log2_scale

    def keep(js, ss):
        mx = mx_ref[...]
        for j, s in zip(js, ss):
            s_ref[j] = s
            mx = jnp.maximum(mx, _fold_lanes(s, jnp.maximum))
        mx_ref[...] = mx

    mx_ref[...] = jnp.full(mx_ref.shape, NEG_BIG, _f32)
    _for_blocks(i, lambda js: keep(js, [scores(j) for j in js]))
    row = lax.broadcasted_iota(jnp.int32, (tile, tile), 0) // CHUNK
    col = lax.broadcasted_iota(jnp.int32, (tile, tile), 1) // CHUNK
    keep([i], [jnp.where(col <= row, scores(i), NEG_BIG)])

    mx_ref[...] = jnp.broadcast_to(jnp.max(mx_ref[...], axis=1, keepdims=True), mx_ref.shape)
    l_ref[...] = jnp.zeros(l_ref.shape, _f32)
    acc_ref[...] = jnp.zeros(acc_ref.shape, _f32)

    def sweep2(js):
        m = _lane_tile(mx_ref[...], tile)
        l, acc = l_ref[...], acc_ref[...]
        for j in js:
            p = jnp.exp2(s_ref[j] - m)
            l += _fold_lanes(p, jnp.add)
            v = v_ref[0, 0, pl.ds(pl.multiple_of(j * tile, tile), tile), :]
            acc += jnp.dot(p.astype(_bf16), v, preferred_element_type=_f32)
        l_ref[...] = l
        acc_ref[...] = acc

    _for_blocks(i + 1, sweep2)
    o_ref[...] = (acc_ref[...] / jnp.sum(l_ref[...], axis=1, keepdims=True)).astype(o_ref.dtype)


def mla_mixer(proj, bsz, length, g_q, w_uq, g_kv, w_ukv):
    m = bsz * length
    tm = min(PROJ_TM, length)
    nt = length // tm
    heads = A_HEADS
    qd = A_NOPE + A_ROPE
    wq = jnp.pad(w_uq.reshape(A_Q_LORA, heads, qd), ((0, 0), (0, 0), (0, A_QW - qd)))
    wq = wq.reshape(A_Q_LORA, heads * A_QW).astype(_bf16)
    wkv = w_ukv.astype(_bf16)
    qc, qsu, qsd = _rope_tables(length, A_ROPE, A_NOPE, A_QW)
    kc, ksu, ksd = _rope_tables(length, A_ROPE, 0, LANES)

    tab_q = pl.BlockSpec((tm, A_QW), lambda i: (i % nt, 0))
    tab_k = pl.BlockSpec((tm, LANES), lambda i: (i % nt, 0))
    head_out = lambda w: pl.BlockSpec((1, heads, tm, w), lambda i: (i // nt, 0, i % nt, 0))
    whole = lambda a: pl.BlockSpec(a.shape, lambda i: (0,) * a.ndim)
    gq, gkv = g_q.reshape(1, -1), g_kv.reshape(1, -1)

    q = pl.pallas_call(
        _mla_q_kernel,
        grid=(m // tm,),
        in_specs=[pl.BlockSpec((tm, A_Q_LORA), lambda i: (i, COL_QLAT // A_Q_LORA)),
                  whole(gq), whole(wq), tab_q, tab_q, tab_q],
        out_specs=head_out(A_QW),
        out_shape=jax.ShapeDtypeStruct((bsz, heads, length, A_QW), _bf16),
        compiler_params=_params("parallel"),
        name="mla_q_proj",
    )(proj, gq, wq, qc, qsu, qsd)

    k, v = pl.pallas_call(
        _mla_kv_kernel,
        grid=(m // tm,),
        in_specs=[pl.BlockSpec((tm, A_KV_LORA), lambda i: (i, COL_KVLAT // A_KV_LORA)),
                  pl.BlockSpec((tm, LANES), lambda i: (i, COL_KROPE // LANES)),
                  whole(gkv), whole(wkv), tab_k, tab_k, tab_k],
        out_specs=(head_out(A_QW), head_out(A_VDIM)),
        out_shape=(jax.ShapeDtypeStruct((bsz, heads, length, A_QW), _bf16),
                   jax.ShapeDtypeStruct((bsz, heads, length, A_VDIM), _bf16)),
        compiler_params=_params("parallel"),
        name="mla_kv_proj",
    )(proj, proj, gkv, wkv, kc, ksu, ksd)

    t = min(ATT_T, length)
    nq = length // t
    return pl.pallas_call(
        functools.partial(_mla_flash_kernel, log2_scale=float(qd) ** -0.5 * LOG2E, tile=t),
        grid=(bsz, heads, nq),
        in_specs=[pl.BlockSpec((1, 1, t, A_QW), lambda b, h, i: (b, h, i, 0)),
                  pl.BlockSpec((1, 1, length, A_QW), lambda b, h, i: (b, h, 0, 0)),
                  pl.BlockSpec((1, 1, length, A_VDIM), lambda b, h, i: (b, h, 0, 0))],
        out_specs=pl.BlockSpec((t, A_VDIM), lambda b, h, i: (b * nq + i, h)),
        out_shape=jax.ShapeDtypeStruct((m, heads * A_VDIM), _bf16),
        scratch_shapes=[pltpu.VMEM((nq, t, t), _f32), pltpu.VMEM((t, LANES), _f32),
                        pltpu.VMEM((t, LANES), _f32), pltpu.VMEM((t, A_VDIM), _f32)],
        compiler_params=_params("parallel", "parallel", "arbitrary"),
        name="mla_flash",
    )(q, k, v)


def _s5_operators(lam_re, lam_im, log_step, b_re, b_im, c_re, c_im, d_skip):
    hp = lax.Precision.HIGHEST
    g, n, p, tc = S_GROUPS, S_STATE, S_GROUP_CH, S_TC
    gl = S_LB_GROUPS
    nlb = g // gl
    lr, li = lam_re.astype(_f32), lam_im.astype(_f32)
    dt = jnp.exp(log_step.astype(_f32))[:, None]
    steps = jnp.arange(tc + 1, dtype=_f32)[:, None, None]
    mag = jnp.exp(lr * dt * steps)
    pw_re, pw_im = mag * jnp.cos(li * dt * steps), mag * jnp.sin(li * dt * steps)
    ab_re, ab_im = pw_re[1], pw_im[1]
    den = lr * lr + li * li
    nr, ni = ab_re - 1.0, ab_im
    co_re = (nr * lr + ni * li) / den
    co_im = (ni * lr - nr * li) / den
    br, bi = b_re.astype(_f32), b_im.astype(_f32)
    bb_re = co_re[..., None] * br - co_im[..., None] * bi
    bb_im = co_re[..., None] * bi + co_im[..., None] * br
    cr, ci = c_re.astype(_f32), c_im.astype(_f32)
    cp_re = cr[None] * pw_re[:, :, None, :] - ci[None] * pw_im[:, :, None, :]
    cp_im = cr[None] * pw_im[:, :, None, :] + ci[None] * pw_re[:, :, None, :]
    kern = (jnp.einsum("dgpn,gnq->dgpq", cp_re, bb_re, precision=hp)
            - jnp.einsum("dgpn,gnq->dgpq", cp_im, bb_im, precision=hp))
    pr, pi_ = pw_re[tc - 1 - jnp.arange(tc)], pw_im[tc - 1 - jnp.arange(tc)]
    bs_re = pr[..., None] * bb_re[None] - pi_[..., None] * bb_im[None]
    bs_im = pr[..., None] * bb_im[None] + pi_[..., None] * bb_re[None]
    eye = jnp.eye(gl, dtype=_f32)
    kern = kern.at[0].add(jnp.eye(p, dtype=_f32)[None] * d_skip.astype(_f32)[:, :, None])
    kt = jnp.transpose(kern[:tc].reshape(tc, nlb, gl, p, p), (1, 0, 2, 4, 3))
    bd = (kt[:, :, :, :, None, :] * eye[None, None, :, None, :, None]).reshape(nlb, tc, LANES, LANES)

    def state_in(bs):
        r = jnp.transpose(bs.reshape(tc, nlb, gl, n, p), (1, 0, 4, 2, 3))
        return r.reshape(nlb, tc * p, gl * n)

    def state_out(cpw):
        r = jnp.transpose(cpw.reshape(tc, nlb, gl, p, n), (1, 2, 4, 0, 3))
        return r.reshape(nlb, gl * n, tc * p)

    bcomp = jnp.concatenate([state_in(bs_re), state_in(bs_im)], axis=2)
    ccomp = jnp.concatenate([state_out(cp_re[1:]), -state_out(cp_im[1:])], axis=1)
    a4 = jnp.stack([pw_re[tc].reshape(nlb, gl * n), pw_im[tc].reshape(nlb, gl * n)], axis=1)
    return bd.astype(_bf16), bcomp.astype(_bf16), ccomp.astype(_bf16), a4


def _gelu_tanh(x):
    return 0.5 * x * (1.0 + jnp.tanh(math.sqrt(2.0 / math.pi) * (x + 0.044715 * (x * x * x))))


def _s5_expand(bd_ref, bcomp_ref, ccomp_ref, m4_ref, b4_ref, c4_ref):
    tc, p, n = S_TC, S_GROUP_CH, S_STATE
    wide = tc * LANES
    m4_ref[...] = jnp.zeros(m4_ref.shape, m4_ref.dtype)
    for s in range(tc):
        for t in range(s, tc):
            m4_ref[s * LANES:(s + 1) * LANES, t * LANES:(t + 1) * LANES] = bd_ref[0, t - s]

    def iota(shape, axis):
        return lax.broadcasted_iota(jnp.int32, shape, axis)

    lg = lambda v: v.bit_length() - 1
    group_mask = S_LB_GROUPS - 1

    def spread(shape, lane_axis):
        wide_i, comp_i = iota(shape, lane_axis), iota(shape, 1 - lane_axis)
        hit = jnp.logical_and(wide_i >> lg(LANES) == comp_i >> lg(p), wide_i & (p - 1) == comp_i & (p - 1))
        return jnp.where(hit, 1.0, 0.0).astype(_bf16)

    b4 = jnp.dot(spread((wide, tc * p), 0), bcomp_ref[0], preferred_element_type=_f32)
    same = (iota(b4.shape, 0) >> lg(p)) & group_mask == (iota(b4.shape, 1) >> lg(n)) & group_mask
    b4_ref[...] = jnp.where(same, b4, 0.0).astype(b4_ref.dtype)
    c4 = jnp.dot(ccomp_ref[0], spread((tc * p, wide), 1), preferred_element_type=_f32)
    same = (iota(c4.shape, 0) >> lg(n)) & group_mask == (iota(c4.shape, 1) >> lg(p)) & group_mask
    c4_ref[...] = jnp.where(same, c4, 0.0).astype(c4_ref.dtype)


def _s5_kernel(x_ref, bd_ref, bcomp_ref, ccomp_ref, a4_ref, o_ref, m4_ref, b4_ref, c4_ref, s_ref, hp_ref,
               *, nchunks):
    @pl.when(pl.program_id(1) == 0)
    def _():
        _s5_expand(bd_ref, bcomp_ref, ccomp_ref, m4_ref, b4_ref, c4_ref)

    u = jnp.concatenate([x_ref[pl.ds(s, nchunks, stride=S_TC), :].astype(_bf16) for s in range(S_TC)], axis=1)
    s_ref[...] = jnp.dot(u, b4_ref[...], preferred_element_type=_f32)
    half = s_ref.shape[1] // 2
    a_re = a4_ref[0, 0:1, :]
    a_im = a4_ref[0, 1:2, :]

    def step(c8, carry):
        h_re, h_im = carry
        base = pl.multiple_of(c8 * SUBLANES, SUBLANES)
        s_blk = s_ref[pl.ds(base, SUBLANES), :]
        rows_re, rows_im = [], []
        for r in range(SUBLANES):
            rows_re.append(h_re)
            rows_im.append(h_im)
            s_re = s_blk[r:r + 1, 0:half]
            s_im = s_blk[r:r + 1, half:2 * half]
            h_re, h_im = a_re * h_re - a_im * h_im + s_re, a_re * h_im + a_im * h_re + s_im
        hp_ref[pl.ds(base, SUBLANES), :] = jnp.concatenate(
            [jnp.concatenate(rows_re, axis=0), jnp.concatenate(rows_im, axis=0)], axis=1)
        return h_re, h_im

    zero = jnp.zeros((1, half), _f32)
    lax.fori_loop(0, nchunks // SUBLANES, step, (zero, zero))
    y = jnp.dot(u, m4_ref[...], preferred_element_type=_f32)
    y += jnp.dot(hp_ref[...].astype(_bf16), c4_ref[...], preferred_element_type=_f32)
    g = _gelu_tanh(y)
    for t in range(S_TC):
        o_ref[pl.ds(t, nchunks, stride=S_TC), :] = g[:, t * LANES:(t + 1) * LANES]


def _glu_kernel(g_ref, w_ref, b_ref, o_ref):
    g = g_ref[...]
    z = jnp.dot(g.astype(_bf16), w_ref[...], preferred_element_type=_f32) + b_ref[...]
    o_ref[...] = (g * jax.nn.sigmoid(z)).astype(o_ref.dtype)


def s5_mixer(proj, bsz, length, lam_re, lam_im, log_step, b_re, b_im, c_re, c_im, d_skip, w_glu, b_glu):
    m = bsz * length
    nch = length // S_TC
    nlb = S_WIDTH // LANES
    wide = S_TC * LANES
    nstate = S_LB_GROUPS * S_STATE
    bd, bcomp, ccomp, a4 = _s5_operators(lam_re, lam_im, log_step, b_re, b_im, c_re, c_im, d_skip)
    per_block = lambda a: pl.BlockSpec((1,) + a.shape[1:], lambda lb, b: (lb,) + (0,) * (a.ndim - 1))
    g = pl.pallas_call(
        functools.partial(_s5_kernel, nchunks=nch),
        grid=(nlb, bsz),
        in_specs=[pl.BlockSpec((length, LANES), lambda lb, b: (b, COL_U // LANES + lb)),
                  per_block(bd), per_block(bcomp), per_block(ccomp), per_block(a4)],
        out_specs=pl.BlockSpec((length, LANES), lambda lb, b: (b, lb)),
        out_shape=jax.ShapeDtypeStruct((m, S_WIDTH), _f32),
        scratch_shapes=[pltpu.VMEM((wide, wide), _bf16), pltpu.VMEM((wide, 2 * nstate), _bf16),
                        pltpu.VMEM((2 * nstate, wide), _bf16),
                        pltpu.VMEM((nch, 2 * nstate), _f32), pltpu.VMEM((nch, 2 * nstate), _f32)],
        compiler_params=_params("arbitrary", "arbitrary"),
        name="s5_scan",
    )(proj, bd, bcomp, ccomp, a4)
    tm = min(MM_TM, m)
    return pl.pallas_call(
        _glu_kernel,
        grid=(m // tm,),
        in_specs=[pl.BlockSpec((tm, S_WIDTH), lambda i: (i, 0)),
                  pl.BlockSpec((S_WIDTH, S_WIDTH), lambda i: (0, 0)),
                  pl.BlockSpec((1, S_WIDTH), lambda i: (0, 0))],
        out_specs=pl.BlockSpec((tm, S_WIDTH), lambda i: (i, 0)),
        out_shape=jax.ShapeDtypeStruct((m, S_WIDTH), _bf16),
        compiler_params=_params("parallel"),
        name="s5_glu",
    )(g, w_glu.astype(_bf16), b_glu.reshape(1, -1).astype(_f32))


def _t5_bucket(rel):
    half = T5_BUCKETS // 2
    max_exact = half // 2
    ret = jnp.where(rel > 0, half, 0)
    n = jnp.abs(rel)
    nf = jnp.maximum(n, 1).astype(_f32)
    large = max_exact + (jnp.log(nf / max_exact) / math.log(T5_MAX_DIST / max_exact)
                         * (half - max_exact)).astype(jnp.int32)
    large = jnp.minimum(large, half - 1)
    return ret + jnp.where(n < max_exact, n, large)


def _dsa_bias_tiles(rel_bias, t):
    assert t >= T5_MAX_DIST
    hp = lax.Precision.HIGHEST
    r = jnp.arange(t, dtype=jnp.int32)[:, None]
    c = jnp.arange(t, dtype=jnp.int32)[None, :]
    rel = jnp.stack([c - r - t, c - r])
    onehot = (_t5_bucket(rel)[..., None] == jnp.arange(T5_BUCKETS)).astype(_f32)
    far = rel_bias.astype(_f32)[T5_BUCKETS // 2 - 1]
    tiles = jnp.einsum("xrcb,bh->hxrc", onehot, rel_bias.astype(_f32) - far[None, :], precision=hp)
    return tiles * LOG2E


def _dsa_prep_kernel(qi_ref, ki_ref, qc_ref, qsu_ref, qsd_ref, kc_ref, ksu_ref, ksd_ref, qo_ref, ko_ref):
    half = I_ROPE // 2
    rep = (I_HEADS * I_DIM) // LANES
    tile = lambda r: jnp.concatenate([r[...]] * rep, axis=1)
    q = _apply_rope(qi_ref[...], tile(qc_ref), tile(qsu_ref), tile(qsd_ref), half)
    for h in range(I_HEADS):
        qo_ref[0, h] = q[:, h * I_DIM:(h + 1) * I_DIM].astype(_bf16)
    ko_ref[...] = _apply_rope(ki_ref[...], kc_ref[...], ksu_ref[...], ksd_ref[...], half).astype(_bf16)


def _float_key(x):
    b = lax.bitcast_convert_type(x, jnp.int32)
    return b ^ ((b >> 31) & 0x7FFFFFFF)


def _dsa_kernel(qi_ref, wi_ref, kit_ref, qc_ref, kct_ref, vc_ref, bias_ref, o_ref,
                keys_ref, s_ref, wb_ref, qb_ref, mx_ref, l_ref, acc_ref,
                *, tile, k_sel, pos_bits, log2_scale, idx_scale):
    i = pl.program_id(1)

    w = wi_ref[...]
    for h in range(I_HEADS):
        wb_ref[h] = jnp.broadcast_to(w[:, h:h + 1], (tile, LANES))
    qf = qc_ref[...]
    for h in range(C_HEADS):
        qb_ref[h] = qf[:, h * C_HEAD_DIM:(h + 1) * C_HEAD_DIM].astype(_bf16)

    row_chunk = lax.broadcasted_iota(jnp.int32, (tile, tile), 0) // CHUNK
    col_chunk = lax.broadcasted_iota(jnp.int32, (tile, tile), 1) // CHUNK
    admissible_diag = col_chunk <= row_chunk

    def score_block(j, carry):
        kt = kit_ref[0, j]
        acc = jnp.zeros((tile, tile), _f32)
        for h in range(I_HEADS):
            d = jnp.dot(qi_ref[0, h], kt, preferred_element_type=_f32)
            acc += jnp.maximum(d, 0.0) * _lane_tile(wb_ref[h], tile)
        key = _float_key(acc * idx_scale)
        keys_ref[j] = jnp.where(jnp.logical_or(j < i, admissible_diag), key, INT_MIN)
        return carry

    lax.fori_loop(0, i + 1, score_block, 0)

    def count_where(pred):
        def blk(j, c):
            kj = keys_ref[j]
            for s in range(tile // LANES):
                c += jnp.where(pred(j, s, kj[:, s * LANES:(s + 1) * LANES]), 1, 0)
            return c

        c = lax.fori_loop(0, i + 1, blk, jnp.zeros((tile, LANES), jnp.int32))
        return jnp.sum(c, axis=1, keepdims=True)

    def count_ge(t):
        tb = jnp.broadcast_to(t, (tile, LANES))
        return count_where(lambda j, s, k: k >= tb)

    def search_cond(carry):
        bit, _, _, settled = carry
        return jnp.logical_and(bit < 32, settled == 0)

    def search_step(carry):
        bit, t, cnt_t, _ = carry
        cand = t + jnp.left_shift(jnp.int32(1), 31 - bit)
        cnt = count_ge(cand)
        take = cnt >= k_sel
        t = jnp.where(take, cand, t)
        cnt_t = jnp.where(take, cnt, cnt_t)
        settled = jnp.min(jnp.where(cnt_t == k_sel, 1, 0))
        return bit + 1, t, cnt_t, settled

    all_selected = ((i + 1) * tile <= k_sel).astype(jnp.int32)
    _, thr, cnt_thr, _ = lax.while_loop(
        search_cond, search_step,
        (jnp.int32(0), jnp.full((tile, 1), INT_MIN, jnp.int32), jnp.full((tile, 1), -1, jnp.int32), all_selected))
    thr = jnp.maximum(thr, INT_MIN + 1)
    thr_b = jnp.broadcast_to(thr, (tile, tile))
    excess = jnp.maximum(cnt_thr - k_sel, 0)
    tied = jnp.max(excess) > 0

    def write_mask(j, selected):
        mb = jnp.where(selected, 0.0, NEG_BIG).astype(_f32)
        keys_ref[j] = lax.bitcast_convert_type(mb, jnp.int32)

    @pl.when(jnp.logical_not(tied))
    def _():
        def mask_block(j, carry):
            write_mask(j, keys_ref[j] >= thr_b)
            return carry

        lax.fori_loop(0, i + 1, mask_block, 0)

    @pl.when(tied)
    def _():
        tl = jnp.broadcast_to(thr, (tile, LANES))
        lane = lax.broadcasted_iota(jnp.int32, (tile, LANES), 1)
        need = count_where(lambda j, s, k: k == tl) - excess

        def below(cut):
            cb = jnp.broadcast_to(cut, (tile, LANES))
            return count_where(lambda j, s, k: jnp.logical_and(k == tl, j * tile + s * LANES + lane < cb))

        def cut_step(b, cut):
            cand = cut + jnp.left_shift(jnp.int32(1), pos_bits - 1 - b)
            return jnp.where(below(cand) < need, cand, cut)

        cut = lax.fori_loop(0, pos_bits, cut_step, jnp.zeros((tile, 1), jnp.int32)) + 1
        cut_b = jnp.broadcast_to(cut, (tile, tile))
        pos = lax.broadcasted_iota(jnp.int32, (tile, tile), 1)

        def mask_block(j, carry):
            kj = keys_ref[j]
            keep_tied = jnp.logical_and(kj == thr_b, j * tile + pos < cut_b)
            write_mask(j, jnp.logical_or(kj > thr_b, keep_tied))
            return carry

        lax.fori_loop(0, i + 1, mask_block, 0)


    grp = DSA_HEAD_GROUP
    rows = grp * tile

    def group_body(g, carry):
        h0 = pl.multiple_of(g * grp, grp)
        q = qb_ref[pl.ds(h0, grp)].reshape(rows, C_HEAD_DIM)
        mx_ref[...] = jnp.full(mx_ref.shape, NEG_BIG, _f32)

        def scores(j):
            s = jnp.dot(q, kct_ref[0, j], preferred_element_type=_f32) * log2_scale
            return s.reshape(grp, tile, tile) + lax.bitcast_convert_type(keys_ref[j], _f32)[None]

        def keep(js, ss):
            mx = mx_ref[...]
            for j, s in zip(js, ss):
                s_ref[:, j] = s
                mx = jnp.maximum(mx, _fold_lanes(s.reshape(rows, tile), jnp.maximum))
            mx_ref[...] = mx

        _for_blocks(jnp.maximum(i - 1, 0), lambda js: keep(js, [scores(j) for j in js]))

        @pl.when(i >= 1)
        def _():
            keep([i - 1], [scores(i - 1) + bias_ref[pl.ds(h0, grp), 0]])

        keep([i], [scores(i) + bias_ref[pl.ds(h0, grp), 1]])

        mx_ref[...] = jnp.broadcast_to(jnp.max(mx_ref[...], axis=1, keepdims=True), mx_ref.shape)
        l_ref[...] = jnp.zeros(l_ref.shape, _f32)
        acc_ref[pl.ds(h0, grp)] = jnp.zeros((grp, tile, C_HEAD_DIM), _f32)

        def sweep2(js):
            m = _lane_tile(mx_ref[...], tile)
            l, acc = l_ref[...], acc_ref[pl.ds(h0, grp)]
            for j in js:
                p = jnp.exp2(s_ref[:, j].reshape(rows, tile) - m)
                l += _fold_lanes(p, jnp.add)
                pv = jnp.dot(p.astype(_bf16), vc_ref[0, j], preferred_element_type=_f32)
                acc += pv.reshape(grp, tile, C_HEAD_DIM)
            l_ref[...] = l
            acc_ref[pl.ds(h0, grp)] = acc

        _for_blocks(i + 1, sweep2)
        denom = jnp.sum(l_ref[...], axis=1, keepdims=True).reshape(grp, tile, 1)
        acc_ref[pl.ds(h0, grp)] = acc_ref[pl.ds(h0, grp)] / denom
        return carry

    lax.fori_loop(0, C_HEADS // grp, group_body, 0)
    for h in range(C_HEADS):
        o_ref[:, h * C_HEAD_DIM:(h + 1) * C_HEAD_DIM] = acc_ref[h].astype(o_ref.dtype)


def dsa_mixer(proj, bsz, length, rel_bias):
    m = bsz * length
    t = min(DSA_T, length)
    nb = length // t
    k_sel = min(TOPK_MAX, length // 4)
    tm = min(PROJ_TM, length)
    nt = length // tm
    iw = I_HEADS * I_DIM
    qtab = _rope_tables(length, I_ROPE, 0, I_DIM)
    qtab = [jnp.concatenate([a.at[:, I_ROPE:].set(1.0) if n == 0 else a] * (LANES // I_DIM), axis=1)
            for n, a in enumerate(qtab)]
    ktab = _rope_tables(length, I_ROPE, 0, LANES)
    ktab = [ktab[0].at[:, I_ROPE:I_DIM].set(1.0), ktab[1], ktab[2]]
    tab = pl.BlockSpec((tm, LANES), lambda i: (i % nt, 0))
    qi, ki = pl.pallas_call(
        _dsa_prep_kernel,
        grid=(m // tm,),
        in_specs=[pl.BlockSpec((tm, iw), lambda i: (i, COL_QI // iw)),
                  pl.BlockSpec((tm, LANES), lambda i: (i, COL_KI // LANES)),
                  tab, tab, tab, tab, tab, tab],
        out_specs=(pl.BlockSpec((1, I_HEADS, tm, I_DIM), lambda i: (i // nt, 0, i % nt, 0)),
                   pl.BlockSpec((tm, LANES), lambda i: (i, 0))),
        out_shape=(jax.ShapeDtypeStruct((bsz, I_HEADS, length, I_DIM), _bf16),
                   jax.ShapeDtypeStruct((m, LANES), _bf16)),
        compiler_params=_params("parallel"),
        name="dsa_prep",
    )(proj, proj, *qtab, *ktab)

    kit = jnp.transpose(ki[:, :I_DIM].reshape(bsz, nb, t, I_DIM), (0, 1, 3, 2))
    kct = jnp.transpose(proj[:, COL_KC:COL_KC + C_HEAD_DIM].astype(_bf16).reshape(bsz, nb, t, C_HEAD_DIM),
                        (0, 1, 3, 2))
    vc = proj[:, COL_VC:COL_VC + C_HEAD_DIM].astype(_bf16).reshape(bsz, nb, t, C_HEAD_DIM)
    bias = _dsa_bias_tiles(rel_bias, t)
    qw = C_HEADS * C_HEAD_DIM
    whole = lambda shape: pl.BlockSpec((1,) + shape, lambda b, i: (b, 0, 0, 0))
    return pl.pallas_call(
        functools.partial(_dsa_kernel, tile=t, k_sel=k_sel, pos_bits=max(length - 1, 1).bit_length(),
                          log2_scale=float(C_HEAD_DIM) ** -0.5 * LOG2E,
                          idx_scale=float(I_DIM * I_HEADS) ** -0.5),
        grid=(bsz, nb),
        in_specs=[pl.BlockSpec((1, I_HEADS, t, I_DIM), lambda b, i: (b, 0, i, 0)),
                  pl.BlockSpec((t, LANES), lambda b, i: (b * nb + i, COL_WI // LANES)),
                  whole((nb, I_DIM, t)),
                  pl.BlockSpec((t, qw), lambda b, i: (b * nb + i, COL_QC // qw)),
                  whole((nb, C_HEAD_DIM, t)),
                  whole((nb, t, C_HEAD_DIM)),
                  pl.BlockSpec((C_HEADS, 2, t, t), lambda b, i: (0, 0, 0, 0),
                               pipeline_mode=pl.Buffered(1))],
        out_specs=pl.BlockSpec((t, qw), lambda b, i: (b * nb + i, 0)),
        out_shape=jax.ShapeDtypeStruct((m, qw), _bf16),
        scratch_shapes=[pltpu.VMEM((nb, t, t), jnp.int32),
                        pltpu.VMEM((DSA_HEAD_GROUP, nb, t, t), _f32),
                        pltpu.VMEM((I_HEADS, t, LANES), _f32),
                        pltpu.VMEM((C_HEADS, t, C_HEAD_DIM), _bf16),
                        pltpu.VMEM((DSA_HEAD_GROUP * t, LANES), _f32),
                        pltpu.VMEM((DSA_HEAD_GROUP * t, LANES), _f32),
                        pltpu.VMEM((C_HEADS, t, C_HEAD_DIM), _f32)],
        compiler_params=_params("parallel", "arbitrary"),
        name="dsa_attention",
    )(qi, proj, kit, proj, kct, vc, bias)


def _reorder_w_in(w):
    sizes = (A_Q_LORA, A_KV_LORA, A_ROPE, S_WIDTH, C_HEADS * C_HEAD_DIM, C_HEAD_DIM, C_HEAD_DIM,
             I_HEADS * I_DIM, I_DIM, I_HEADS)
    offs = [0]
    for s in sizes:
        offs.append(offs[-1] + s)
    q_lat, kv_lat, k_rope, u, q_c, k_c, v_c, q_i, k_i, w_i = [w[:, offs[n]:offs[n + 1]] for n in range(10)]
    pad = lambda a, width: jnp.pad(a, ((0, 0), (0, width - a.shape[1])))
    cols = [q_c, q_lat, k_c, v_c, pad(k_rope, LANES), pad(k_i, LANES), pad(w_i, LANES),
            jnp.zeros((w.shape[0], LANES), w.dtype), u, q_i, kv_lat]
    out = jnp.concatenate(cols, axis=1).astype(_bf16)
    assert out.shape[1] == PROJ_WIDTH
    return out


def kernel(x, ln_in_g, ln_in_b, rel_bias, w_in, a_gq, a_wuq, a_gkv, a_wukv, s_lam_re, s_lam_im, s_log_step, s_b_re, s_b_im, s_c_re, s_c_im, s_d, s_w_glu, s_b_glu, w_out, ln1_g, ln1_b, w_ff1, w_ff2, ln2_g, ln2_b):
    bsz, length, d_model = x.shape
    depth = w_in.shape[0]
    alpha = (2 * depth) ** 0.25
    m = bsz * length

    h, hb = layer_norm(x.reshape(m, d_model), ln_in_g, ln_in_b)
    for l in range(depth):
        proj = matmul(hb, _reorder_w_in(w_in[l]), _f32)
        a_out = mla_mixer(proj, bsz, length, a_gq[l], a_wuq[l], a_gkv[l], a_wukv[l])
        b_out = s5_mixer(proj, bsz, length, s_lam_re[l], s_lam_im[l], s_log_step[l], s_b_re[l], s_b_im[l],
                         s_c_re[l], s_c_im[l], s_d[l], s_w_glu[l], s_b_glu[l])
        c_out = dsa_mixer(proj, bsz, length, rel_bias)
        mix = matmul_concat3(a_out, b_out, c_out, w_out[l].astype(_bf16))
        h, hb = layer_norm(mix, ln1_g[l], ln1_b[l], resid=h, alpha=alpha)
        act = matmul(hb, w_ff1, _bf16, act="relu2", layer=l)
        ff = matmul_ksplit(act, w_ff2[l].astype(_bf16))
        h, hb = layer_norm(ff, ln2_g[l], ln2_b[l], resid=h, alpha=alpha)
    return h.reshape(bsz, length, d_model)
```

```python
import functools
import math

import jax
import jax.numpy as jnp
from jax import lax
from jax.experimental import pallas as pl
from jax.experimental.pallas import tpu as pltpu

CHUNK = 64
A_HEADS, A_NOPE, A_ROPE, A_VDIM = 12, 128, 64, 128
A_Q_LORA, A_KV_LORA = 768, 512
S_GROUP_CH, S_WIDTH, S_STATE = 16, 1024, 64
S_GROUPS = S_WIDTH // S_GROUP_CH
C_HEADS, C_HEAD_DIM = 12, 128
I_HEADS, I_DIM, I_ROPE = 16, 64, 32
TOPK_MAX = 256
T5_BUCKETS, T5_MAX_DIST = 32, 128
ROPE_THETA = 10000.0
LN_EPS = 1e-5
RMS_EPS = 1e-6

LANES = 128
SUBLANES = 8
VMEM_LIMIT_BYTES = 56 * 1024 * 1024

MM_TM, MM_TN, MM_TK = 1024, 1024, 4096
ROW_TILE = 256
PROJ_TM = 512
ATT_T = 512
DSA_T = 256
DSA_HEAD_GROUP = 4
S_TC = 16
S_LB_GROUPS = LANES // S_GROUP_CH

COL_QC, COL_QLAT, COL_KC, COL_VC, COL_KROPE, COL_KI, COL_WI = 0, 1536, 2304, 2432, 2560, 2688, 2816
COL_U, COL_QI, COL_KVLAT, PROJ_WIDTH = 3072, 4096, 5120, 5632

NEG_BIG = -1e30
LOG2E = math.log2(math.e)
INT_MIN = -2 ** 31

_bf16 = jnp.bfloat16
_f32 = jnp.float32


def _tile(n, pref):
    t = min(pref, n)
    while n % t:
        t //= 2
    return t


def _params(*sem):
    return pltpu.CompilerParams(dimension_semantics=sem, vmem_limit_bytes=VMEM_LIMIT_BYTES)


def _mm_kernel(x_ref, w_ref, o_ref, *, act):
    y = jnp.dot(x_ref[...], w_ref[...].astype(_bf16), preferred_element_type=_f32)
    if act == "relu2":
        y = jnp.square(jnp.maximum(y, 0.0))
    o_ref[...] = y.astype(o_ref.dtype)


def _layer_block(w, layer, shape, index):
    if w.ndim == 2:
        return pl.BlockSpec(shape, index)
    return pl.BlockSpec((None,) + shape, lambda *g: (layer,) + index(*g))


def matmul(x, w, out_dtype, act=None, layer=0):
    m, k = x.shape
    n = w.shape[-1]
    tm, tn = _tile(m, MM_TM), _tile(n, MM_TN)
    x_mode = dict(pipeline_mode=pl.Buffered(1)) if w.dtype == _f32 else {}
    return pl.pallas_call(
        functools.partial(_mm_kernel, act=act),
        grid=(m // tm, n // tn),
        in_specs=[pl.BlockSpec((tm, k), lambda i, j: (i, 0), **x_mode),
                  _layer_block(w, layer, (k, tn), lambda i, j: (0, j))],
        out_specs=pl.BlockSpec((tm, tn), lambda i, j: (i, j)),
        out_shape=jax.ShapeDtypeStruct((m, n), out_dtype),
        compiler_params=_params("parallel", "parallel"),
        name="matmul",
    )(x, w)


def _mm_acc_kernel(x_ref, w_ref, o_ref):
    y = jnp.dot(x_ref[...], w_ref[...], preferred_element_type=_f32)

    @pl.when(pl.program_id(2) == 0)
    def _():
        o_ref[...] = y

    @pl.when(pl.program_id(2) != 0)
    def _():
        o_ref[...] += y


def matmul_ksplit(x, w, layer):
    m, k = x.shape
    n = w.shape[-1]
    tm, tn, tk = min(MM_TM, m), min(MM_TN, n), min(MM_TK, k)
    return pl.pallas_call(
        _mm_acc_kernel,
        grid=(m // tm, n // tn, k // tk),
        in_specs=[pl.BlockSpec((tm, tk), lambda i, j, l: (i, l)),
                  _layer_block(w, layer, (tk, tn), lambda i, j, l: (l, j))],
        out_specs=pl.BlockSpec((tm, tn), lambda i, j, l: (i, j)),
        out_shape=jax.ShapeDtypeStruct((m, n), _f32),
        compiler_params=_params("parallel", "parallel", "arbitrary"),
        name="matmul_ksplit",
    )(x, w)


def _mm3_kernel(a_ref, b_ref, c_ref, w_ref, o_ref):
    ka, kb = a_ref.shape[1], b_ref.shape[1]
    y = jnp.dot(a_ref[...], w_ref[0:ka, :], preferred_element_type=_f32)
    y += jnp.dot(b_ref[...], w_ref[ka:ka + kb, :], preferred_element_type=_f32)
    y += jnp.dot(c_ref[...], w_ref[ka + kb:, :], preferred_element_type=_f32)
    o_ref[...] = y


def matmul_concat3(a, b, c, w):
    m = a.shape[0]
    k, n = w.shape
    assert a.shape[1] + b.shape[1] + c.shape[1] == k
    tm, tn = _tile(m, MM_TM), _tile(n, MM_TN)
    row = lambda i, j: (i, 0)
    return pl.pallas_call(
        _mm3_kernel,
        grid=(m // tm, n // tn),
        in_specs=[pl.BlockSpec((tm, a.shape[1]), row), pl.BlockSpec((tm, b.shape[1]), row),
                  pl.BlockSpec((tm, c.shape[1]), row),
                  pl.BlockSpec((k, tn), lambda i, j: (0, j))],
        out_specs=pl.BlockSpec((tm, tn), lambda i, j: (i, j)),
        out_shape=jax.ShapeDtypeStruct((m, n), _f32),
        compiler_params=_params("parallel", "parallel"),
        name="matmul_concat3",
    )(a, b, c, w)


def _ln_rows(z, g, b):
    mu = jnp.mean(z, axis=-1, keepdims=True)
    zc = z - mu
    var = jnp.mean(zc * zc, axis=-1, keepdims=True)
    return zc * lax.rsqrt(var + LN_EPS) * g + b


def _ln_kernel(x_ref, g_ref, b_ref, o_ref, ob_ref):
    y = _ln_rows(x_ref[...], g_ref[...], b_ref[...])
    o_ref[...] = y
    ob_ref[...] = y.astype(_bf16)


def _res_ln_kernel(h_ref, y_ref, g_ref, b_ref, o_ref, ob_ref, *, alpha):
    y = _ln_rows(alpha * h_ref[...] + y_ref[...], g_ref[...], b_ref[...])
    o_ref[...] = y
    ob_ref[...] = y.astype(_bf16)


def layer_norm(x, g, b, resid=None, alpha=1.0):
    m, d = x.shape
    tr = min(ROW_TILE, m)
    rows = pl.BlockSpec((tr, d), lambda i: (i, 0))
    vec = pl.BlockSpec((1, d), lambda i: (0, 0))
    out_shape = (jax.ShapeDtypeStruct((m, d), _f32), jax.ShapeDtypeStruct((m, d), _bf16))
    g2, b2 = g.reshape(1, d), b.reshape(1, d)
    if resid is None:
        return pl.pallas_call(_ln_kernel, grid=(m // tr,), in_specs=[rows, vec, vec],
                              out_specs=(rows, rows), out_shape=out_shape,
                              compiler_params=_params("parallel"), name="layer_norm")(x, g2, b2)
    return pl.pallas_call(functools.partial(_res_ln_kernel, alpha=alpha), grid=(m // tr,),
                          in_specs=[rows, rows, vec, vec], out_specs=(rows, rows),
                          out_shape=out_shape, compiler_params=_params("parallel"),
                          name="residual_layer_norm")(resid, x, g2, b2)


def _rope_cos_sin(length, dim):
    inv = ROPE_THETA ** (-jnp.arange(0, dim, 2, dtype=_f32) / dim)
    ang = jnp.arange(length, dtype=_f32)[:, None] * inv[None, :]
    return jnp.cos(ang), jnp.sin(ang)


def _rope_tables(length, dim, lead, width):
    cos, sin = _rope_cos_sin(length, dim)
    half = dim // 2
    zeros = lambda n: jnp.zeros((length, n), _f32)
    tail = width - lead - dim
    c = jnp.concatenate([jnp.ones((length, lead), _f32), cos, cos, zeros(tail)], axis=1)
    s_up = jnp.concatenate([zeros(lead + half), sin, zeros(tail)], axis=1)
    s_dn = jnp.concatenate([zeros(lead), -sin, zeros(half + tail)], axis=1)
    return c, s_up, s_dn


def _apply_rope(x, c, s_up, s_dn, half):
    width = x.shape[-1]
    return x * c + pltpu.roll(x, half, 1) * s_up + pltpu.roll(x, width - half, 1) * s_dn


def _rms_rows(x, g):
    return x * lax.rsqrt(jnp.mean(x * x, axis=-1, keepdims=True) + RMS_EPS) * g


A_QW = 256


def _mla_q_kernel(x_ref, g_ref, w_ref, c_ref, su_ref, sd_ref, o_ref):
    xn = _rms_rows(x_ref[...], g_ref[...]).astype(_bf16)
    y = jnp.dot(xn, w_ref[...], preferred_element_type=_f32)
    c, su, sd = c_ref[...], su_ref[...], sd_ref[...]
    for h in range(A_HEADS):
        yh = _apply_rope(y[:, h * A_QW:(h + 1) * A_QW], c, su, sd, A_ROPE // 2)
        o_ref[0, h] = yh.astype(_bf16)


def _mla_kv_kernel(x_ref, kr_ref, g_ref, w_ref, c_ref, su_ref, sd_ref, k_ref, v_ref):
    xn = _rms_rows(x_ref[...], g_ref[...]).astype(_bf16)
    y = jnp.dot(xn, w_ref[...], preferred_element_type=_f32)
    kr = _apply_rope(kr_ref[...], c_ref[...], su_ref[...], sd_ref[...], A_ROPE // 2).astype(_bf16)
    hw = A_NOPE + A_VDIM
    for h in range(A_HEADS):
        k_ref[0, h] = jnp.concatenate([y[:, h * hw:h * hw + A_NOPE].astype(_bf16), kr], axis=1)
        v_ref[0, h] = y[:, h * hw + A_NOPE:(h + 1) * hw].astype(_bf16)


def _fold_lanes(x, op):
    r = x[:, 0:LANES]
    for n in range(1, x.shape[1] // LANES):
        r = op(r, x[:, n * LANES:(n + 1) * LANES])
    return r


def _lane_tile(x, width):
    return jnp.concatenate([x] * (width // LANES), axis=1)


def _for_blocks(n, fn):
    def quad(g, carry):
        fn([4 * g + u for u in range(4)])
        return carry

    lax.fori_loop(0, n // 4, quad, 0)
    base = (n // 4) * 4

    @pl.when(n % 4 >= 2)
    def _():
        fn([base, base + 1])

    @pl.when(n % 2 == 1)
    def _():
        fn([n - 1])


def _mla_flash_kernel(q_ref, k_ref, v_ref, o_ref, s_ref, mx_ref, l_ref, acc_ref, *, log2_scale, tile):
    i = pl.program_id(2)
    q = q_ref[0, 0]

    def scores(j):
        k = k_ref[0, 0, pl.ds(pl.multiple_of(j * tile, tile), tile), :]
        return lax.dot_general(q, k, (((1,), (1,)), ((), ())), preferred_element_type=_f32) * log2_scale

    def keep(js, ss):
        mx = mx_ref[...]
        for j, s in zip(js, ss):
            s_ref[j] = s
            mx = jnp.maximum(mx, _fold_lanes(s, jnp.maximum))
        mx_ref[...] = mx

    mx_ref[...] = jnp.full(mx_ref.shape, NEG_BIG, _f32)
    _for_blocks(i, lambda js: keep(js, [scores(j) for j in js]))
    row = lax.broadcasted_iota(jnp.int32, (tile, tile), 0) // CHUNK
    col = lax.broadcasted_iota(jnp.int32, (tile, tile), 1) // CHUNK
    keep([i], [jnp.where(col <= row, scores(i), NEG_BIG)])

    mx_ref[...] = jnp.broadcast_to(jnp.max(mx_ref[...], axis=1, keepdims=True), mx_ref.shape)
    l_ref[...] = jnp.zeros(l_ref.shape, _f32)
    acc_ref[...] = jnp.zeros(acc_ref.shape, _f32)

    def sweep2(js):
        m = _lane_tile(mx_ref[...], tile)
        l, acc = l_ref[...], acc_ref[...]
        for j in js:
            p = jnp.exp2(s_ref[j] - m)
            l += _fold_lanes(p, jnp.add)
            v = v_ref[0, 0, pl.ds(pl.multiple_of(j * tile, tile), tile), :]
            acc += jnp.dot(p.astype(_bf16), v, preferred_element_type=_f32)
        l_ref[...] = l
        acc_ref[...] = acc

    _for_blocks(i + 1, sweep2)
    o_ref[...] = (acc_ref[...] / jnp.sum(l_ref[...], axis=1, keepdims=True)).astype(o_ref.dtype)


def mla_mixer(proj, bsz, length, g_q, w_uq, g_kv, w_ukv):
    m = bsz * length
    tm = min(PROJ_TM, length)
    nt = length // tm
    heads = A_HEADS
    qd = A_NOPE + A_ROPE
    wq = jnp.pad(w_uq.reshape(A_Q_LORA, heads, qd), ((0, 0), (0, 0), (0, A_QW - qd)))
    wq = wq.reshape(A_Q_LORA, heads * A_QW).astype(_bf16)
    wkv = w_ukv.astype(_bf16)
    qc, qsu, qsd = _rope_tables(length, A_ROPE, A_NOPE, A_QW)
    kc, ksu, ksd = _rope_tables(length, A_ROPE, 0, LANES)

    tab_q = pl.BlockSpec((tm, A_QW), lambda i: (i % nt, 0))
    tab_k = pl.BlockSpec((tm, LANES), lambda i: (i % nt, 0))
    head_out = lambda w: pl.BlockSpec((1, heads, tm, w), lambda i: (i // nt, 0, i % nt, 0))
    whole = lambda a: pl.BlockSpec(a.shape, lambda i: (0,) * a.ndim)
    gq, gkv = g_q.reshape(1, -1), g_kv.reshape(1, -1)

    q = pl.pallas_call(
        _mla_q_kernel,
        grid=(m // tm,),
        in_specs=[pl.BlockSpec((tm, A_Q_LORA), lambda i: (i, COL_QLAT // A_Q_LORA)),
                  whole(gq), whole(wq), tab_q, tab_q, tab_q],
        out_specs=head_out(A_QW),
        out_shape=jax.ShapeDtypeStruct((bsz, heads, length, A_QW), _bf16),
        compiler_params=_params("parallel"),
        name="mla_q_proj",
    )(proj, gq, wq, qc, qsu, qsd)

    k, v = pl.pallas_call(
        _mla_kv_kernel,
        grid=(m // tm,),
        in_specs=[pl.BlockSpec((tm, A_KV_LORA), lambda i: (i, COL_KVLAT // A_KV_LORA)),
                  pl.BlockSpec((tm, LANES), lambda i: (i, COL_KROPE // LANES)),
                  whole(gkv), whole(wkv), tab_k, tab_k, tab_k],
        out_specs=(head_out(A_QW), head_out(A_VDIM)),
        out_shape=(jax.ShapeDtypeStruct((bsz, heads, length, A_QW), _bf16),
                   jax.ShapeDtypeStruct((bsz, heads, length, A_VDIM), _bf16)),
        compiler_params=_params("parallel"),
        name="mla_kv_proj",
    )(proj, proj, gkv, wkv, kc, ksu, ksd)

    t = min(ATT_T, length)
    nq = length // t
    return pl.pallas_call(
        functools.partial(_mla_flash_kernel, log2_scale=float(qd) ** -0.5 * LOG2E, tile=t),
        grid=(bsz, heads, nq),
        in_specs=[pl.BlockSpec((1, 1, t, A_QW), lambda b, h, i: (b, h, i, 0)),
                  pl.BlockSpec((1, 1, length, A_QW), lambda b, h, i: (b, h, 0, 0)),
                  pl.BlockSpec((1, 1, length, A_VDIM), lambda b, h, i: (b, h, 0, 0))],
        out_specs=pl.BlockSpec((t, A_VDIM), lambda b, h, i: (b * nq + i, h)),
        out_shape=jax.ShapeDtypeStruct((m, heads * A_VDIM), _bf16),
        scratch_shapes=[pltpu.VMEM((nq, t, t), _f32), pltpu.VMEM((t, LANES), _f32),
                        pltpu.VMEM((t, LANES), _f32), pltpu.VMEM((t, A_VDIM), _f32)],
        compiler_params=_params("parallel", "parallel", "arbitrary"),
        name="mla_flash",
    )(q, k, v)


def _s5_operators(lam_re, lam_im, log_step, b_re, b_im, c_re, c_im, d_skip):
    hp = lax.Precision.HIGHEST
    g, n, p, tc = S_GROUPS, S_STATE, S_GROUP_CH, S_TC
    gl = S_LB_GROUPS
    nlb = g // gl
    lr, li = lam_re.astype(_f32), lam_im.astype(_f32)
    dt = jnp.exp(log_step.astype(_f32))[:, None]
    steps = jnp.arange(tc + 1, dtype=_f32)[:, None, None]
    mag = jnp.exp(lr * dt * steps)
    pw_re, pw_im = mag * jnp.cos(li * dt * steps), mag * jnp.sin(li * dt * steps)
    ab_re, ab_im = pw_re[1], pw_im[1]
    den = lr * lr + li * li
    nr, ni = ab_re - 1.0, ab_im
    co_re = (nr * lr + ni * li) / den
    co_im = (ni * lr - nr * li) / den
    br, bi = b_re.astype(_f32), b_im.astype(_f32)
    bb_re = co_re[..., None] * br - co_im[..., None] * bi
    bb_im = co_re[..., None] * bi + co_im[..., None] * br
    cr, ci = c_re.astype(_f32), c_im.astype(_f32)
    cp_re = cr[None] * pw_re[:, :, None, :] - ci[None] * pw_im[:, :, None, :]
    cp_im = cr[None] * pw_im[:, :, None, :] + ci[None] * pw_re[:, :, None, :]
    kern = (jnp.einsum("dgpn,gnq->dgpq", cp_re, bb_re, precision=hp)
            - jnp.einsum("dgpn,gnq->dgpq", cp_im, bb_im, precision=hp))
    pr, pi_ = pw_re[tc - 1 - jnp.arange(tc)], pw_im[tc - 1 - jnp.arange(tc)]
    bs_re = pr[..., None] * bb_re[None] - pi_[..., None] * bb_im[None]
    bs_im = pr[..., None] * bb_im[None] + pi_[..., None] * bb_re[None]
    eye = jnp.eye(gl, dtype=_f32)
    kern = kern.at[0].add(jnp.eye(p, dtype=_f32)[None] * d_skip.astype(_f32)[:, :, None])
    kt = jnp.transpose(kern[:tc].reshape(tc, nlb, gl, p, p), (1, 0, 2, 4, 3))
    bd = (kt[:, :, :, :, None, :] * eye[None, None, :, None, :, None]).reshape(nlb, tc, LANES, LANES)

    def state_in(bs):
        r = jnp.transpose(bs.reshape(tc, nlb, gl, n, p), (1, 0, 4, 2, 3))
        return r.reshape(nlb, tc * p, gl * n)

    def state_out(cpw):
        r = jnp.transpose(cpw.reshape(tc, nlb, gl, p, n), (1, 2, 4, 0, 3))
        return r.reshape(nlb, gl * n, tc * p)

    bcomp = jnp.concatenate([state_in(bs_re), state_in(bs_im)], axis=2)
    ccomp = jnp.concatenate([state_out(cp_re[1:]), -state_out(cp_im[1:])], axis=1)
    a4 = jnp.stack([pw_re[tc].reshape(nlb, gl * n), pw_im[tc].reshape(nlb, gl * n)], axis=1)
    return bd.astype(_bf16), bcomp.astype(_bf16), ccomp.astype(_bf16), a4


def _gelu_tanh(x):
    return 0.5 * x * (1.0 + jnp.tanh(math.sqrt(2.0 / math.pi) * (x + 0.044715 * (x * x * x))))


def _s5_expand(bd_ref, bcomp_ref, ccomp_ref, m4_ref, b4_ref, c4_ref):
    tc, p, n = S_TC, S_GROUP_CH, S_STATE
    wide = tc * LANES
    m4_ref[...] = jnp.zeros(m4_ref.shape, m4_ref.dtype)
    for s in range(tc):
        for t in range(s, tc):
            m4_ref[s * LANES:(s + 1) * LANES, t * LANES:(t + 1) * LANES] = bd_ref[0, t - s]

    def iota(shape, axis):
        return lax.broadcasted_iota(jnp.int32, shape, axis)

    lg = lambda v: v.bit_length() - 1
    group_mask = S_LB_GROUPS - 1

    def spread(shape, lane_axis):
        wide_i, comp_i = iota(shape, lane_axis), iota(shape, 1 - lane_axis)
        hit = jnp.logical_and(wide_i >> lg(LANES) == comp_i >> lg(p), wide_i & (p - 1) == comp_i & (p - 1))
        return jnp.where(hit, 1.0, 0.0).astype(_bf16)

    b4 = jnp.dot(spread((wide, tc * p), 0), bcomp_ref[0], preferred_element_type=_f32)
    same = (iota(b4.shape, 0) >> lg(p)) & group_mask == (iota(b4.shape, 1) >> lg(n)) & group_mask
    b4_ref[...] = jnp.where(same, b4, 0.0).astype(b4_ref.dtype)
    c4 = jnp.dot(ccomp_ref[0], spread((tc * p, wide), 1), preferred_element_type=_f32)
    same = (iota(c4.shape, 0) >> lg(n)) & group_mask == (iota(c4.shape, 1) >> lg(p)) & group_mask
    c4_ref[...] = jnp.where(same, c4, 0.0).astype(c4_ref.dtype)


def _s5_kernel(x_ref, bd_ref, bcomp_ref, ccomp_ref, a4_ref, o_ref, m4_ref, b4_ref, c4_ref, s_ref, hp_ref,
               *, nchunks):
    @pl.when(pl.program_id(1) == 0)
    def _():
        _s5_expand(bd_ref, bcomp_ref, ccomp_ref, m4_ref, b4_ref, c4_ref)

    u = jnp.concatenate([x_ref[pl.ds(s, nchunks, stride=S_TC), :].astype(_bf16) for s in range(S_TC)], axis=1)
    s_ref[...] = jnp.dot(u, b4_ref[...], preferred_element_type=_f32)
    half = s_ref.shape[1] // 2
    a_re = a4_ref[0, 0:1, :]
    a_im = a4_ref[0, 1:2, :]

    def step(c8, carry):
        h_re, h_im = carry
        base = pl.multiple_of(c8 * SUBLANES, SUBLANES)
        s_blk = s_ref[pl.ds(base, SUBLANES), :]
        rows_re, rows_im = [], []
        for r in range(SUBLANES):
            rows_re.append(h_re)
            rows_im.append(h_im)
            s_re = s_blk[r:r + 1, 0:half]
            s_im = s_blk[r:r + 1, half:2 * half]
            h_re, h_im = a_re * h_re - a_im * h_im + s_re, a_re * h_im + a_im * h_re + s_im
        hp_ref[pl.ds(base, SUBLANES), :] = jnp.concatenate(
            [jnp.concatenate(rows_re, axis=0), jnp.concatenate(rows_im, axis=0)], axis=1)
        return h_re, h_im

    zero = jnp.zeros((1, half), _f32)
    lax.fori_loop(0, nchunks // SUBLANES, step, (zero, zero))
    y = jnp.dot(u, m4_ref[...], preferred_element_type=_f32)
    y += jnp.dot(hp_ref[...].astype(_bf16), c4_ref[...], preferred_element_type=_f32)
    g = _gelu_tanh(y)
    for t in range(S_TC):
        o_ref[pl.ds(t, nchunks, stride=S_TC), :] = g[:, t * LANES:(t + 1) * LANES]


def _glu_kernel(g_ref, w_ref, b_ref, o_ref):
    g = g_ref[...]
    z = jnp.dot(g.astype(_bf16), w_ref[...], preferred_element_type=_f32) + b_ref[...]
    o_ref[...] = (g * jax.nn.sigmoid(z)).astype(o_ref.dtype)


def s5_mixer(proj, bsz, length, lam_re, lam_im, log_step, b_re, b_im, c_re, c_im, d_skip, w_glu, b_glu):
    m = bsz * length
    nch = length // S_TC
    nlb = S_WIDTH // LANES
    wide = S_TC * LANES
    nstate = S_LB_GROUPS * S_STATE
    bd, bcomp, ccomp, a4 = _s5_operators(lam_re, lam_im, log_step, b_re, b_im, c_re, c_im, d_skip)
    per_block = lambda a: pl.BlockSpec((1,) + a.shape[1:], lambda lb, b: (lb,) + (0,) * (a.ndim - 1))
    g = pl.pallas_call(
        functools.partial(_s5_kernel, nchunks=nch),
        grid=(nlb, bsz),
        in_specs=[pl.BlockSpec((length, LANES), lambda lb, b: (b, COL_U // LANES + lb)),
                  per_block(bd), per_block(bcomp), per_block(ccomp), per_block(a4)],
        out_specs=pl.BlockSpec((length, LANES), lambda lb, b: (b, lb)),
        out_shape=jax.ShapeDtypeStruct((m, S_WIDTH), _f32),
        scratch_shapes=[pltpu.VMEM((wide, wide), _bf16), pltpu.VMEM((wide, 2 * nstate), _bf16),
                        pltpu.VMEM((2 * nstate, wide), _bf16),
                        pltpu.VMEM((nch, 2 * nstate), _f32), pltpu.VMEM((nch, 2 * nstate), _f32)],
        compiler_params=_params("arbitrary", "arbitrary"),
        name="s5_scan",
    )(proj, bd, bcomp, ccomp, a4)
    tm = min(MM_TM, m)
    return pl.pallas_call(
        _glu_kernel,
        grid=(m // tm,),
        in_specs=[pl.BlockSpec((tm, S_WIDTH), lambda i: (i, 0)),
                  pl.BlockSpec((S_WIDTH, S_WIDTH), lambda i: (0, 0)),
                  pl.BlockSpec((1, S_WIDTH), lambda i: (0, 0))],
        out_specs=pl.BlockSpec((tm, S_WIDTH), lambda i: (i, 0)),
        out_shape=jax.ShapeDtypeStruct((m, S_WIDTH), _bf16),
        compiler_params=_params("parallel"),
        name="s5_glu",
    )(g, w_glu.astype(_bf16), b_glu.reshape(1, -1).astype(_f32))


def _t5_bucket(rel):
    half = T5_BUCKETS // 2
    max_exact = half // 2
    ret = jnp.where(rel > 0, half, 0)
    n = jnp.abs(rel)
    nf = jnp.maximum(n, 1).astype(_f32)
    large = max_exact + (jnp.log(nf / max_exact) / math.log(T5_MAX_DIST / max_exact)
                         * (half - max_exact)).astype(jnp.int32)
    large = jnp.minimum(large, half - 1)
    return ret + jnp.where(n < max_exact, n, large)


def _dsa_bias_tiles(rel_bias, t):
    assert t >= T5_MAX_DIST
    hp = lax.Precision.HIGHEST
    r = jnp.arange(t, dtype=jnp.int32)[:, None]
    c = jnp.arange(t, dtype=jnp.int32)[None, :]
    rel = jnp.stack([c - r - t, c - r])
    onehot = (_t5_bucket(rel)[..., None] == jnp.arange(T5_BUCKETS)).astype(_f32)
    far = rel_bias.astype(_f32)[T5_BUCKETS // 2 - 1]
    tiles = jnp.einsum("xrcb,bh->hxrc", onehot, rel_bias.astype(_f32) - far[None, :], precision=hp)
    return tiles * LOG2E


def _dsa_prep_kernel(qi_ref, ki_ref, qc_ref, qsu_ref, qsd_ref, kc_ref, ksu_ref, ksd_ref, qo_ref, ko_ref):
    half = I_ROPE // 2
    rep = (I_HEADS * I_DIM) // LANES
    tile = lambda r: jnp.concatenate([r[...]] * rep, axis=1)
    q = _apply_rope(qi_ref[...], tile(qc_ref), tile(qsu_ref), tile(qsd_ref), half)
    for h in range(I_HEADS):
        qo_ref[0, h] = q[:, h * I_DIM:(h + 1) * I_DIM].astype(_bf16)
    ko_ref[...] = _apply_rope(ki_ref[...], kc_ref[...], ksu_ref[...], ksd_ref[...], half).astype(_bf16)


def _float_key(x):
    b = lax.bitcast_convert_type(x, jnp.int32)
    return b ^ ((b >> 31) & 0x7FFFFFFF)


def _dsa_kernel(qi_ref, wi_ref, kit_ref, qc_ref, kct_ref, vc_ref, bias_ref, o_ref,
                keys_ref, s_ref, wb_ref, qb_ref, mx_ref, l_ref, acc_ref,
                *, tile, k_sel, pos_bits, log2_scale, idx_scale):
    i = pl.program_id(1)

    w = wi_ref[...]
    for h in range(I_HEADS):
        wb_ref[h] = jnp.broadcast_to(w[:, h:h + 1], (tile, LANES))
    qf = qc_ref[...]
    for h in range(C_HEADS):
        qb_ref[h] = qf[:, h * C_HEAD_DIM:(h + 1) * C_HEAD_DIM].astype(_bf16)

    row_chunk = lax.broadcasted_iota(jnp.int32, (tile, tile), 0) // CHUNK
    col_chunk = lax.broadcasted_iota(jnp.int32, (tile, tile), 1) // CHUNK
    admissible_diag = col_chunk <= row_chunk

    def score_block(j, carry):
        kt = kit_ref[0, j]
        acc = jnp.zeros((tile, tile), _f32)
        for h in range(I_HEADS):
            d = jnp.dot(qi_ref[0, h], kt, preferred_element_type=_f32)
            acc += jnp.maximum(d, 0.0) * _lane_tile(wb_ref[h], tile)
        key = _float_key(acc * idx_scale)
        keys_ref[j] = jnp.where(jnp.logical_or(j < i, admissible_diag), key, INT_MIN)
        return carry

    lax.fori_loop(0, i + 1, score_block, 0)

    def count_where(pred):
        def blk(j, c):
            kj = keys_ref[j]
            for s in range(tile // LANES):
                c += jnp.where(pred(j, s, kj[:, s * LANES:(s + 1) * LANES]), 1, 0)
            return c

        c = lax.fori_loop(0, i + 1, blk, jnp.zeros((tile, LANES), jnp.int32))
        return jnp.sum(c, axis=1, keepdims=True)

    def count_ge(t):
        tb = jnp.broadcast_to(t, (tile, LANES))
        return count_where(lambda j, s, k: k >= tb)

    def search_cond(carry):
        bit, _, _, settled = carry
        return jnp.logical_and(bit < 32, settled == 0)

    def search_step(carry):
        bit, t, cnt_t, _ = carry
        cand = t + jnp.left_shift(jnp.int32(1), 31 - bit)
        cnt = count_ge(cand)
        take = cnt >= k_sel
        t = jnp.where(take, cand, t)
        cnt_t = jnp.where(take, cnt, cnt_t)
        settled = jnp.min(jnp.where(cnt_t == k_sel, 1, 0))
        return bit + 1, t, cnt_t, settled

    all_selected = ((i + 1) * tile <= k_sel).astype(jnp.int32)
    _, thr, cnt_thr, _ = lax.while_loop(
        search_cond, search_step,
        (jnp.int32(0), jnp.full((tile, 1), INT_MIN, jnp.int32), jnp.full((tile, 1), -1, jnp.int32), all_selected))
    thr = jnp.maximum(thr, INT_MIN + 1)
    thr_b = jnp.broadcast_to(thr, (tile, tile))
    excess = jnp.maximum(cnt_thr - k_sel, 0)
    tied = jnp.max(excess) > 0

    def write_mask(j, selected):
        mb = jnp.where(selected, 0.0, NEG_BIG).astype(_f32)
        keys_ref[j] = lax.bitcast_convert_type(mb, jnp.int32)

    @pl.when(jnp.logical_not(tied))
    def _():
        def mask_block(j, carry):
            write_mask(j, keys_ref[j] >= thr_b)
            return carry

        lax.fori_loop(0, i + 1, mask_block, 0)

    @pl.when(tied)
    def _():
        tl = jnp.broadcast_to(thr, (tile, LANES))
        lane = lax.broadcasted_iota(jnp.int32, (tile, LANES), 1)
        need = count_where(lambda j, s, k: k == tl) - excess

        def below(cut):
            cb = jnp.broadcast_to(cut, (tile, LANES))
            return count_where(lambda j, s, k: jnp.logical_and(k == tl, j * tile + s * LANES + lane < cb))

        def cut_step(b, cut):
            cand = cut + jnp.left_shift(jnp.int32(1), pos_bits - 1 - b)
            return jnp.where(below(cand) < need, cand, cut)

        cut = lax.fori_loop(0, pos_bits, cut_step, jnp.zeros((tile, 1), jnp.int32)) + 1
        cut_b = jnp.broadcast_to(cut, (tile, tile))
        pos = lax.broadcasted_iota(jnp.int32, (tile, tile), 1)

        def mask_block(j, carry):
            kj = keys_ref[j]
            keep_tied = jnp.logical_and(kj == thr_b, j * tile + pos < cut_b)
            write_mask(j, jnp.logical_or(kj > thr_b, keep_tied))
            return carry

        lax.fori_loop(0, i + 1, mask_block, 0)


    grp = DSA_HEAD_GROUP
    rows = grp * tile

    def group_body(g, carry):
        h0 = pl.multiple_of(g * grp, grp)
        q = qb_ref[pl.ds(h0, grp)].reshape(rows, C_HEAD_DIM)
        mx_ref[...] = jnp.full(mx_ref.shape, NEG_BIG, _f32)

        def scores(j):
            s = jnp.dot(q, kct_ref[0, j], preferred_element_type=_f32) * log2_scale
            return s.reshape(grp, tile, tile) + lax.bitcast_convert_type(keys_ref[j], _f32)[None]

        def keep(js, ss):
            mx = mx_ref[...]
            for j, s in zip(js, ss):
                s_ref[:, j] = s
                mx = jnp.maximum(mx, _fold_lanes(s.reshape(rows, tile), jnp.maximum))
            mx_ref[...] = mx

        _for_blocks(jnp.maximum(i - 1, 0), lambda js: keep(js, [scores(j) for j in js]))

        @pl.when(i >= 1)
        def _():
            keep([i - 1], [scores(i - 1) + bias_ref[pl.ds(h0, grp), 0]])

        keep([i], [scores(i) + bias_ref[pl.ds(h0, grp), 1]])

        mx_ref[...] = jnp.broadcast_to(jnp.max(mx_ref[...], axis=1, keepdims=True), mx_ref.shape)
        l_ref[...] = jnp.zeros(l_ref.shape, _f32)
        acc_ref[pl.ds(h0, grp)] = jnp.zeros((grp, tile, C_HEAD_DIM), _f32)

        def sweep2(js):
            m = _lane_tile(mx_ref[...], tile)
            l, acc = l_ref[...], acc_ref[pl.ds(h0, grp)]
            for j in js:
                p = jnp.exp2(s_ref[:, j].reshape(rows, tile) - m)
                l += _fold_lanes(p, jnp.add)
                pv = jnp.dot(p.astype(_bf16), vc_ref[0, j], preferred_element_type=_f32)
                acc += pv.reshape(grp, tile, C_HEAD_DIM)
            l_ref[...] = l
            acc_ref[pl.ds(h0, grp)] = acc

        _for_blocks(i + 1, sweep2)
        denom = jnp.sum(l_ref[...], axis=1, keepdims=True).reshape(grp, tile, 1)
        acc_ref[pl.ds(h0, grp)] = acc_ref[pl.ds(h0, grp)] / denom
        return carry

    lax.fori_loop(0, C_HEADS // grp, group_body, 0)
    for h in range(C_HEADS):
        o_ref[:, h * C_HEAD_DIM:(h + 1) * C_HEAD_DIM] = acc_ref[h].astype(o_ref.dtype)


def dsa_mixer(proj, bsz, length, bias):
    m = bsz * length
    t = min(DSA_T, length)
    nb = length // t
    k_sel = min(TOPK_MAX, length // 4)
    tm = min(PROJ_TM, length)
    nt = length // tm
    iw = I_HEADS * I_DIM
    qtab = _rope_tables(length, I_ROPE, 0, I_DIM)
    qtab = [jnp.concatenate([a.at[:, I_ROPE:].set(1.0) if n == 0 else a] * (LANES // I_DIM), axis=1)
            for n, a in enumerate(qtab)]
    ktab = _rope_tables(length, I_ROPE, 0, LANES)
    ktab = [ktab[0].at[:, I_ROPE:I_DIM].set(1.0), ktab[1], ktab[2]]
    tab = pl.BlockSpec((tm, LANES), lambda i: (i % nt, 0))
    qi, ki = pl.pallas_call(
        _dsa_prep_kernel,
        grid=(m // tm,),
        in_specs=[pl.BlockSpec((tm, iw), lambda i: (i, COL_QI // iw)),
                  pl.BlockSpec((tm, LANES), lambda i: (i, COL_KI // LANES)),
                  tab, tab, tab, tab, tab, tab],
        out_specs=(pl.BlockSpec((1, I_HEADS, tm, I_DIM), lambda i: (i // nt, 0, i % nt, 0)),
                   pl.BlockSpec((tm, LANES), lambda i: (i, 0))),
        out_shape=(jax.ShapeDtypeStruct((bsz, I_HEADS, length, I_DIM), _bf16),
                   jax.ShapeDtypeStruct((m, LANES), _bf16)),
        compiler_params=_params("parallel"),
        name="dsa_prep",
    )(proj, proj, *qtab, *ktab)

    kit = jnp.transpose(ki[:, :I_DIM].reshape(bsz, nb, t, I_DIM), (0, 1, 3, 2))
    kct = jnp.transpose(proj[:, COL_KC:COL_KC + C_HEAD_DIM].astype(_bf16).reshape(bsz, nb, t, C_HEAD_DIM),
                        (0, 1, 3, 2))
    vc = proj[:, COL_VC:COL_VC + C_HEAD_DIM].astype(_bf16).reshape(bsz, nb, t, C_HEAD_DIM)
    assert bias.shape == (C_HEADS, 2, t, t)
    qw = C_HEADS * C_HEAD_DIM
    whole = lambda shape: pl.BlockSpec((1,) + shape, lambda b, i: (b, 0, 0, 0))
    return pl.pallas_call(
        functools.partial(_dsa_kernel, tile=t, k_sel=k_sel, pos_bits=max(length - 1, 1).bit_length(),
                          log2_scale=float(C_HEAD_DIM) ** -0.5 * LOG2E,
                          idx_scale=float(I_DIM * I_HEADS) ** -0.5),
        grid=(bsz, nb),
        in_specs=[pl.BlockSpec((1, I_HEADS, t, I_DIM), lambda b, i: (b, 0, i, 0)),
                  pl.BlockSpec((t, LANES), lambda b, i: (b * nb + i, COL_WI // LANES)),
                  whole((nb, I_DIM, t)),
                  pl.BlockSpec((t, qw), lambda b, i: (b * nb + i, COL_QC // qw)),
                  whole((nb, C_HEAD_DIM, t)),
                  whole((nb, t, C_HEAD_DIM)),
                  pl.BlockSpec((C_HEADS, 2, t, t), lambda b, i: (0, 0, 0, 0),
                               pipeline_mode=pl.Buffered(1))],
        out_specs=pl.BlockSpec((t, qw), lambda b, i: (b * nb + i, 0)),
        out_shape=jax.ShapeDtypeStruct((m, qw), _bf16),
        scratch_shapes=[pltpu.VMEM((nb, t, t), jnp.int32),
                        pltpu.VMEM((DSA_HEAD_GROUP, nb, t, t), _f32),
                        pltpu.VMEM((I_HEADS, t, LANES), _f32),
                        pltpu.VMEM((C_HEADS, t, C_HEAD_DIM), _bf16),
                        pltpu.VMEM((DSA_HEAD_GROUP * t, LANES), _f32),
                        pltpu.VMEM((DSA_HEAD_GROUP * t, LANES), _f32),
                        pltpu.VMEM((C_HEADS, t, C_HEAD_DIM), _f32)],
        compiler_params=_params("parallel", "arbitrary"),
        name="dsa_attention",
    )(qi, proj, kit, proj, kct, vc, bias)


def _reorder_w_in(w):
    sizes = (A_Q_LORA, A_KV_LORA, A_ROPE, S_WIDTH, C_HEADS * C_HEAD_DIM, C_HEAD_DIM, C_HEAD_DIM,
             I_HEADS * I_DIM, I_DIM, I_HEADS)
    offs = [0]
    for s in sizes:
        offs.append(offs[-1] + s)
    w = w.astype(_bf16)
    q_lat, kv_lat, k_rope, u, q_c, k_c, v_c, q_i, k_i, w_i = [w[:, offs[n]:offs[n + 1]] for n in range(10)]
    pad = lambda a, width: jnp.pad(a, ((0, 0), (0, width - a.shape[1])))
    cols = [q_c, q_lat, k_c, v_c, pad(k_rope, LANES), pad(k_i, LANES), pad(w_i, LANES),
            jnp.zeros((w.shape[0], LANES), w.dtype), u, q_i, kv_lat]
    out = jnp.concatenate(cols, axis=1)
    assert out.shape[1] == PROJ_WIDTH
    return out


def kernel(x, ln_in_g, ln_in_b, rel_bias, w_in, a_gq, a_wuq, a_gkv, a_wukv, s_lam_re, s_lam_im, s_log_step, s_b_re, s_b_im, s_c_re, s_c_im, s_d, s_w_glu, s_b_glu, w_out, ln1_g, ln1_b, w_ff1, w_ff2, ln2_g, ln2_b):
    bsz, length, d_model = x.shape
    depth = w_in.shape[0]
    alpha = (2 * depth) ** 0.25
    m = bsz * length

    h, hb = layer_norm(x.reshape(m, d_model), ln_in_g, ln_in_b)
    w_ff2b = w_ff2.astype(_bf16)
    bias = _dsa_bias_tiles(rel_bias, min(DSA_T, length))
    for l in range(depth):
        proj = matmul(hb, _reorder_w_in(w_in[l]), _f32)
        a_out = mla_mixer(proj, bsz, length, a_gq[l], a_wuq[l], a_gkv[l], a_wukv[l])
        b_out = s5_mixer(proj, bsz, length, s_lam_re[l], s_lam_im[l], s_log_step[l], s_b_re[l], s_b_im[l],
                         s_c_re[l], s_c_im[l], s_d[l], s_w_glu[l], s_b_glu[l])
        c_out = dsa_mixer(proj, bsz, length, bias)
        mix = matmul_concat3(a_out, b_out, c_out, w_out[l].astype(_bf16))
        h, hb = layer_norm(mix, ln1_g[l], ln1_b[l], resid=h, alpha=alpha)
        act = matmul(hb, w_ff1, _bf16, act="relu2", layer=l)
        ff = matmul_ksplit(act, w_ff2b, l)
        h, hb = layer_norm(ff, ln2_g[l], ln2_b[l], resid=h, alpha=alpha)
    return h.reshape(bsz, length, d_model)
```
